```python
import math, functools
import jax, jax.numpy as jnp
from jax import lax
import numpy as np

D_MODEL = 1024
BATCH = 8
SEQ = 8192
DEPTH = 1
DEC_BATCH = 128
DEC_SEQ = 1
PAST_LEN = 8192
PAGE_SIZE = 128

N_HEADS = 4
DK = 64
DV = 2 * DK
QK_W = N_HEADS * 2 * DK
V_W = N_HEADS * DV
CONV_CH = 512
CONV_K = 31
D_MIX = V_W + CONV_CH
D_IN = 2 * QK_W + V_W + 2 * CONV_CH
D_FF = 4 * D_MODEL
ROPE_THETA = 10000.0
Q_BLOCK = 128
ATTN_SCALE = DK ** -0.5
NORM_EPS = 1e-6
NEG_INF = -1e30

kernel_name = "hymba_diffattn_conformer_step"


def _rms(x, g):
    xf = x.astype(jnp.float32)
    y = xf * lax.rsqrt(jnp.mean(xf * xf, axis=-1, keepdims=True) + NORM_EPS)
    return (y * g.astype(jnp.float32)).astype(x.dtype)


def _layernorm(x, g, b):
    xf = x.astype(jnp.float32)
    mu = jnp.mean(xf, axis=-1, keepdims=True)
    var = jnp.mean(jnp.square(xf - mu), axis=-1, keepdims=True)
    y = (xf - mu) * lax.rsqrt(var + NORM_EPS)
    return (y * g.astype(jnp.float32) + b.astype(jnp.float32)).astype(x.dtype)


def _rope(x, pos):
    half = DK // 2
    inv_freq = ROPE_THETA ** (-jnp.arange(half, dtype=jnp.float32) / half)
    ang = pos.astype(jnp.float32)[:, None] * inv_freq[None, :]
    cos = jnp.cos(ang)[:, None, None, :].astype(x.dtype)
    sin = jnp.sin(ang)[:, None, None, :].astype(x.dtype)
    x1, x2 = x[..., :half], x[..., half:]
    return jnp.concatenate([x1 * cos - x2 * sin, x2 * cos + x1 * sin], axis=-1)


def _diff_attend(q, k, v, mask, lam):
    s = jnp.einsum('bqhcd,bkhcd->bhcqk', q, k, preferred_element_type=jnp.float32) * ATTN_SCALE
    s = jnp.where(mask, s, NEG_INF)
    p = jax.nn.softmax(s, axis=-1)
    a = p[:, :, 0] - lam * p[:, :, 1]
    return jnp.einsum('bhqk,bkhd->bqhd', a.astype(v.dtype), v)


def _attend_prompt(q, k, v, lam):
    B, S = q.shape[0], q.shape[1]
    kpos = jnp.arange(S)

    def block(i):
        start = i * Q_BLOCK
        qb = lax.dynamic_slice_in_dim(q, start, Q_BLOCK, axis=1)
        qpos = start + jnp.arange(Q_BLOCK)
        return _diff_attend(qb, k, v, qpos[:, None] >= kpos[None, :], lam)

    o = lax.map(block, jnp.arange(S // Q_BLOCK))
    return jnp.moveaxis(o, 0, 1).reshape(B, S, N_HEADS, DV)


def _attend_sample(q, k, v, lam, k_past, v_past):
    T, P = q.shape[1], k_past.shape[1]
    kk = jnp.concatenate([k_past, k], axis=1)
    vv = jnp.concatenate([v_past, v], axis=1)
    qpos = P + jnp.arange(T)
    kpos = jnp.arange(P + T)
    return _diff_attend(q, kk, vv, kpos[None, :] <= qpos[:, None], lam)


def _dwconv(u_ext, w, b):
    y = lax.conv_general_dilated(u_ext, w[:, None, :], window_strides=(1,), padding='VALID',
                                 dimension_numbers=('NWC', 'WIO', 'NWC'),
                                 feature_group_count=CONV_CH)
    return y + b


def _mixer(h, pos, attend, conv_left, lam_init, weights):
    (w_in, w_out, lam_q1, lam_k1, lam_q2, lam_k2, g_subln, w_dw, b_dw, g_cln, b_cln) = weights
    B, S, _ = h.shape
    z = h @ w_in
    q, k, v, u = jnp.split(z, [QK_W, 2 * QK_W, 2 * QK_W + V_W], axis=-1)
    q = _rope(q.reshape(B, S, N_HEADS, 2, DK), pos)
    k = _rope(k.reshape(B, S, N_HEADS, 2, DK), pos)
    v = v.reshape(B, S, N_HEADS, DV)
    lam = (jnp.exp(jnp.sum(lam_q1 * lam_k1).astype(jnp.float32))
           - jnp.exp(jnp.sum(lam_q2 * lam_k2).astype(jnp.float32)) + lam_init)
    o = attend(q, k, v, lam)
    o = (_rms(o, g_subln) * (1.0 - lam_init)).reshape(B, S, V_W)
    ua, ug = jnp.split(u, 2, axis=-1)
    u = ua * jax.nn.sigmoid(ug)
    u_ext = jnp.concatenate([conv_left, u], axis=1)
    cv = jax.nn.silu(_layernorm(_dwconv(u_ext, w_dw, b_dw), g_cln, b_cln))
    y = jnp.concatenate([o, cv], axis=-1) @ w_out
    return y, (k.reshape(B, S, N_HEADS, 2 * DK), v, u_ext[:, -(CONV_K - 1):])


def _layer(x, c, mix, g_pre_mix, g_post_mix, g_pre_ffn, g_post_ffn, w_ada, b_ada, w_up, w_down):
    mod = jax.nn.silu(c) @ w_ada + b_ada
    sh1, sc1, gt1, sh2, sc2, gt2 = [t[:, None, :] for t in jnp.split(mod, 6, axis=-1)]
    h = _rms(x, g_pre_mix) * (1.0 + sc1) + sh1
    m, new_state = mix(h)
    x = x + gt1 * _rms(m, g_post_mix)
    h = _rms(x, g_pre_ffn) * (1.0 + sc2) + sh2
    f = jnp.square(jax.nn.relu(h @ w_up)) @ w_down
    x = x + gt2 * _rms(f, g_post_ffn)
    return x, new_state


def setup_inputs(seed: int = 0) -> dict:
    key = jax.random.key(seed)
    ks = jax.random.split(key, 32)
    f32 = jnp.float32

    def nrm(k, shape, scale):
        return jax.random.normal(k, shape, f32) * scale

    def gain(k, shape):
        return 1.0 + 0.05 * jax.random.normal(k, shape, f32)

    n_pages = PAST_LEN // PAGE_SIZE
    n_used = DEC_BATCH * n_pages
    n_phys = n_used + (n_used + 3) // 4
    page_table = jax.random.permutation(ks[7], n_phys)[:n_used].reshape(DEC_BATCH, n_pages).astype(jnp.int32)
    return {
        "x_prompt": nrm(ks[0], (BATCH, SEQ, D_MODEL), 1.0),
        "x_sample": nrm(ks[1], (DEC_BATCH, DEC_SEQ, D_MODEL), 1.0),
        "c_prompt": nrm(ks[2], (BATCH, D_MODEL), 1.0),
        "c_sample": nrm(ks[3], (DEC_BATCH, D_MODEL), 1.0),
        "cache_k": nrm(ks[4], (DEPTH, n_phys, PAGE_SIZE, N_HEADS, 2 * DK), 1.0),
        "cache_v": nrm(ks[5], (DEPTH, n_phys, PAGE_SIZE, N_HEADS, DV), 1.0),
        "state_conv": nrm(ks[6], (DEPTH, DEC_BATCH, CONV_K - 1, CONV_CH), 0.5),
        "page_table": page_table,
        "w_in": nrm(ks[8], (DEPTH, D_MODEL, D_IN), D_MODEL ** -0.5),
        "w_out": nrm(ks[9], (DEPTH, D_MIX, D_MODEL), D_MIX ** -0.5),
        "lam_q1": nrm(ks[10], (DEPTH, DK), 0.1),
        "lam_k1": nrm(ks[11], (DEPTH, DK), 0.1),
        "lam_q2": nrm(ks[12], (DEPTH, DK), 0.1),
        "lam_k2": nrm(ks[13], (DEPTH, DK), 0.1),
        "g_subln": gain(ks[14], (DEPTH, DV)),
        "w_dw": nrm(ks[15], (DEPTH, CONV_K, CONV_CH), CONV_K ** -0.5),
        "b_dw": nrm(ks[16], (DEPTH, CONV_CH), 0.02),
        "g_cln": gain(ks[17], (DEPTH, CONV_CH)),
        "b_cln": nrm(ks[18], (DEPTH, CONV_CH), 0.02),
        "g_pre_mix": gain(ks[19], (DEPTH, D_MODEL)),
        "g_post_mix": gain(ks[20], (DEPTH, D_MODEL)),
        "g_pre_ffn": gain(ks[21], (DEPTH, D_MODEL)),
        "g_post_ffn": gain(ks[22], (DEPTH, D_MODEL)),
        "w_ada": nrm(ks[23], (DEPTH, D_MODEL, 6 * D_MODEL), 0.5 * D_MODEL ** -0.5),
        "b_ada": nrm(ks[24], (DEPTH, 6 * D_MODEL), 0.02),
        "w_up": nrm(ks[25], (DEPTH, D_MODEL, D_FF), D_MODEL ** -0.5),
        "w_down": nrm(ks[26], (DEPTH, D_FF, D_MODEL), D_FF ** -0.5),
    }


def reference(x_prompt, x_sample, c_prompt, c_sample, cache_k, cache_v, state_conv, page_table,
              w_in, w_out, lam_q1, lam_k1, lam_q2, lam_k2, g_subln, w_dw, b_dw, g_cln, b_cln,
              g_pre_mix, g_post_mix, g_pre_ffn, g_post_ffn, w_ada, b_ada, w_up, w_down):
    B, S, _ = x_prompt.shape
    DB, T, _ = x_sample.shape
    past = page_table.shape[1] * cache_k.shape[2]
    pos_p = jnp.arange(S)
    pos_s = past + jnp.arange(T)
    conv_left_p = jnp.zeros((B, CONV_K - 1, CONV_CH), x_prompt.dtype)
    yp, ys = x_prompt, x_sample
    kp_l, vp_l, cp_l, ks_l, vs_l, cs_l = [], [], [], [], [], []
    for l in range(DEPTH):
        lam_init = 0.8 - 0.6 * math.exp(-0.3 * l)
        mw = (w_in[l], w_out[l], lam_q1[l], lam_k1[l], lam_q2[l], lam_k2[l], g_subln[l],
              w_dw[l], b_dw[l], g_cln[l], b_cln[l])
        lw = (g_pre_mix[l], g_post_mix[l], g_pre_ffn[l], g_post_ffn[l], w_ada[l], b_ada[l], w_up[l], w_down[l])
        mix_p = functools.partial(_mixer, pos=pos_p, attend=_attend_prompt, conv_left=conv_left_p,
                                  lam_init=lam_init, weights=mw)
        yp, (kp, vp, cp) = _layer(yp, c_prompt, mix_p, *lw)
        k_past = cache_k[l][page_table].reshape(DB, past, N_HEADS, 2, DK)
        v_past = cache_v[l][page_table].reshape(DB, past, N_HEADS, DV)
        att_s = functools.partial(_attend_sample, k_past=k_past, v_past=v_past)
        mix_s = functools.partial(_mixer, pos=pos_s, attend=att_s, conv_left=state_conv[l],
                                  lam_init=lam_init, weights=mw)
        ys, (ksn, vsn, csn) = _layer(ys, c_sample, mix_s, *lw)
        kp_l.append(kp); vp_l.append(vp); cp_l.append(cp)
        ks_l.append(ksn); vs_l.append(vsn); cs_l.append(csn)
    k_prompt = jnp.stack(kp_l)
    v_prompt = jnp.stack(vp_l)
    conv_prompt = jnp.stack(cp_l)
    k_sample = jnp.stack(ks_l)
    v_sample = jnp.stack(vs_l)
    conv_sample = jnp.stack(cs_l)
    return (yp, ys, k_prompt, v_prompt, conv_prompt, k_sample, v_sample, conv_sample)
```

```python
import functools
import math

import jax
import jax.numpy as jnp
from jax import lax
from jax.experimental import pallas as pl
from jax.experimental.pallas import tpu as pltpu

F32 = jnp.float32
BF16 = jnp.bfloat16

N_HEADS = 4
DK = 64
DV = 2 * DK
HEAD_W = 2 * DK
QK_W = N_HEADS * 2 * DK
V_W = N_HEADS * DV
CONV_CH = 512
CONV_K = 31
ROPE_THETA = 10000.0
ATTN_SCALE = DK ** -0.5
NORM_EPS = 1e-6
NEG_INF = -1e30

LANES = 128
SUBLANES = 8
ROW_TILE = 512
FF_CHUNK = 1024
ADA_COL_TILE = 1024
PAGES_PER_STEP = 8
CONV_HALO = 32
VMEM_LIMIT = 56 * 1024 * 1024


def _rms(x, g):
    return x * lax.rsqrt(jnp.mean(x * x, axis=-1, keepdims=True) + NORM_EPS) * g


def _sigmoid(x):
    return 1.0 / (1.0 + jnp.exp(-x))


def _rope_128(x, cos, sin_signed):
    lane = lax.broadcasted_iota(jnp.int32, x.shape, 1)
    first_half = (lane & (DK - 1)) < (DK // 2)
    partner = jnp.where(first_half, pltpu.roll(x, LANES - DK // 2, 1), pltpu.roll(x, DK // 2, 1))
    return x * cos + partner * sin_signed


def _mlp(h2b, w_up_ref, w_down_ref, d_ff):
    f = None
    for c in range(d_ff // FF_CHUNK):
        hid = jnp.dot(h2b, w_up_ref[:, c * FF_CHUNK:(c + 1) * FF_CHUNK], preferred_element_type=F32)
        hid = jnp.maximum(hid, 0.0)
        hid = (hid * hid).astype(BF16)
        part = jnp.dot(hid, w_down_ref[c * FF_CHUNK:(c + 1) * FF_CHUNK, :], preferred_element_type=F32)
        f = part if f is None else f + part
    return f


def _conv_ln_swish(conv, g_cln, b_cln):
    mu = jnp.mean(conv, axis=-1, keepdims=True)
    cen = conv - mu
    var = jnp.mean(cen * cen, axis=-1, keepdims=True)
    y = cen * lax.rsqrt(var + NORM_EPS) * g_cln + b_cln
    return y * _sigmoid(y)


def _lambda(lq1, lk1, lq2, lk2, lam_init):
    s1 = jnp.sum(lq1 * lk1, axis=-1, keepdims=True)
    s2 = jnp.sum(lq2 * lk2, axis=-1, keepdims=True)
    return jnp.exp(s1) - jnp.exp(s2) + lam_init


def _ada_kernel(c_ref, w_ref, b_ref, o_ref):
    c = c_ref[...]
    s = (c * _sigmoid(c)).astype(BF16)
    o_ref[...] = jnp.dot(s, w_ref[...], preferred_element_type=F32) + b_ref[...]


def _ada(c, w_ada_b, b_ada):
    rows, d = c.shape
    n = w_ada_b.shape[1]
    return pl.pallas_call(
        _ada_kernel,
        grid=(n // ADA_COL_TILE,),
        in_specs=[
            pl.BlockSpec((rows, d), lambda j: (0, 0)),
            pl.BlockSpec((d, ADA_COL_TILE), lambda j: (0, j)),
            pl.BlockSpec((1, ADA_COL_TILE), lambda j: (0, j)),
        ],
        out_specs=pl.BlockSpec((rows, ADA_COL_TILE), lambda j: (0, j)),
        out_shape=jax.ShapeDtypeStruct((rows, n), F32),
        name="ada",
    )(c, w_ada_b, b_ada)


def _inproj_math(x, shift, scale, g_pre, w_in_ref, cos, sin_signed):
    h = _rms(x, g_pre) * (1.0 + scale) + shift
    z = jnp.dot(h.astype(BF16), w_in_ref[...], preferred_element_type=F32)
    qs, ks, vs = [], [], []
    for hh in range(N_HEADS):
        lo = hh * HEAD_W
        qs.append(_rope_128(z[:, lo:lo + HEAD_W], cos, sin_signed) * ATTN_SCALE)
        ks.append(_rope_128(z[:, QK_W + lo:QK_W + lo + HEAD_W], cos, sin_signed))
        vs.append(z[:, 2 * QK_W + lo:2 * QK_W + lo + DV])
    ua = z[:, 2 * QK_W + V_W:2 * QK_W + V_W + CONV_CH]
    ug = z[:, 2 * QK_W + V_W + CONV_CH:]
    u = ua * _sigmoid(ug)
    return qs, ks, vs, u


def _p_inproj_kernel(x_ref, mod_ref, g_ref, w_ref, cos_ref, sin_ref,
                     q_ref, kb_ref, vt_ref, k_ref, v_ref, u_ref):
    d_model = x_ref.shape[-1]
    tm = x_ref.shape[1]
    shift = mod_ref[0, :, 0:d_model]
    scale = mod_ref[0, :, d_model:2 * d_model]
    qs, ks, vs, u = _inproj_math(x_ref[0], shift, scale, g_ref[...], w_ref, cos_ref[...], sin_ref[...])
    for hh in range(N_HEADS):
        q_ref[0, hh] = qs[hh].astype(BF16)
        kb_ref[0, hh] = ks[hh].astype(BF16)
        k_ref[0, pl.ds(hh, tm, stride=N_HEADS), :] = ks[hh]
        v_ref[0, pl.ds(hh, tm, stride=N_HEADS), :] = vs[hh]
        vt_ref[0, hh, 0] = vs[hh].T.astype(BF16)
    u_ref[0] = u


def _p_inproj(x, mod3, g_pre, w_in_b, cos, sin_signed, tm):
    b, s, d = x.shape
    nt = s // tm
    d_in = w_in_b.shape[1]
    return pl.pallas_call(
        _p_inproj_kernel,
        grid=(b, nt),
        in_specs=[
            pl.BlockSpec((1, tm, d), lambda bi, i: (bi, i, 0)),
            pl.BlockSpec((1, 1, mod3.shape[-1]), lambda bi, i: (bi, 0, 0)),
            pl.BlockSpec((1, d), lambda bi, i: (0, 0)),
            pl.BlockSpec((d, d_in), lambda bi, i: (0, 0)),
            pl.BlockSpec((tm, LANES), lambda bi, i: (i, 0)),
            pl.BlockSpec((tm, LANES), lambda bi, i: (i, 0)),
        ],
        out_specs=[
            pl.BlockSpec((1, N_HEADS, tm, HEAD_W), lambda bi, i: (bi, 0, i, 0)),
            pl.BlockSpec((1, N_HEADS, tm, HEAD_W), lambda bi, i: (bi, 0, i, 0)),
            pl.BlockSpec((1, N_HEADS, 1, DV, tm), lambda bi, i: (bi, 0, i, 0, 0)),
            pl.BlockSpec((1, tm * N_HEADS, HEAD_W), lambda bi, i: (bi, i, 0)),
            pl.BlockSpec((1, tm * N_HEADS, DV), lambda bi, i: (bi, i, 0)),
            pl.BlockSpec((1, tm, CONV_CH), lambda bi, i: (bi, i, 0)),
        ],
        out_shape=[
            jax.ShapeDtypeStruct((b, N_HEADS, s, HEAD_W), BF16),
            jax.ShapeDtypeStruct((b, N_HEADS, s, HEAD_W), BF16),
            jax.ShapeDtypeStruct((b, N_HEADS, nt, DV, tm), BF16),
            jax.ShapeDtypeStruct((b, s * N_HEADS, HEAD_W), F32),
            jax.ShapeDtypeStruct((b, s * N_HEADS, DV), F32),
            jax.ShapeDtypeStruct((b, s, CONV_CH), F32),
        ],
        compiler_params=pltpu.CompilerParams(
            dimension_semantics=("parallel", "arbitrary"), vmem_limit_bytes=VMEM_LIMIT),
        name="p_inproj",
    )(x, mod3, g_pre, w_in_b, cos, sin_signed)


def _p_attn_kernel(lq1_ref, lk1_ref, lq2_ref, lk2_ref, g_ref, q_ref, k_ref, vt_ref, o_ref,
                   m_ref, l_ref, acc0_ref, acc1_ref, *, lam_init, tq):
    qi = pl.program_id(2)
    q = q_ref[0, 0]
    lane = lax.broadcasted_iota(jnp.int32, q.shape, 1)
    zero = jnp.zeros_like(q)
    q2 = jnp.concatenate([jnp.where(lane < DK, q, zero), jnp.where(lane >= DK, q, zero)], axis=0)

    m_ref[...] = jnp.full(m_ref.shape, NEG_INF, F32)
    l_ref[...] = jnp.zeros(l_ref.shape, F32)
    acc0_ref[...] = jnp.zeros(acc0_ref.shape, F32)
    acc1_ref[...] = jnp.zeros(acc1_ref.shape, F32)

    def block(c, masked):
        start = pl.multiple_of(c * tq, tq)
        kc = k_ref[0, 0, pl.ds(start, tq), :]
        vtc = vt_ref[0, 0, c]
        st = lax.dot_general(kc, q2, (((1,), (1,)), ((), ())), preferred_element_type=F32)
        if masked:
            row = lax.broadcasted_iota(jnp.int32, st.shape, 0)
            col = lax.broadcasted_iota(jnp.int32, st.shape, 1)
            col = jnp.where(col >= tq, col - tq, col)
            st = jnp.where(row <= col, st, NEG_INF)
        m_old = m_ref[...]
        m_new = jnp.maximum(m_old, jnp.max(st, axis=0, keepdims=True))
        alpha = jnp.exp(m_old - m_new)
        p = jnp.exp(st - m_new)
        l_ref[...] = alpha * l_ref[...] + jnp.sum(p, axis=0, keepdims=True)
        m_ref[...] = m_new
        pb = p.astype(BF16)
        acc0_ref[...] = acc0_ref[...] * alpha[:, :tq] + jnp.dot(vtc, pb[:, :tq], preferred_element_type=F32)
        acc1_ref[...] = acc1_ref[...] * alpha[:, tq:] + jnp.dot(vtc, pb[:, tq:], preferred_element_type=F32)

    def body(c, carry):
        block(c, masked=False)
        return carry

    lax.fori_loop(0, qi, body, 0)
    block(qi, masked=True)

    lam = _lambda(lq1_ref[...], lk1_ref[...], lq2_ref[...], lk2_ref[...], lam_init)
    l = l_ref[...]
    ot = acc0_ref[...] * (1.0 / l[:, :tq]) - lam * (acc1_ref[...] * (1.0 / l[:, tq:]))
    ms = jnp.mean(ot * ot, axis=0, keepdims=True)
    ot = ot * lax.rsqrt(ms + NORM_EPS) * g_ref[...] * (1.0 - lam_init)
    o_ref[0] = ot.T.astype(BF16)


def _p_attn(lam_params, g_col, q, kb, vt, lam_init, tq):
    b, _, s, _ = q.shape
    nq = s // tq
    lam_spec = pl.BlockSpec((1, DK), lambda bi, hi, i: (0, 0))
    return pl.pallas_call(
        functools.partial(_p_attn_kernel, lam_init=lam_init, tq=tq),
        grid=(b, N_HEADS, nq),
        in_specs=[
            lam_spec, lam_spec, lam_spec, lam_spec,
            pl.BlockSpec((DV, 1), lambda bi, hi, i: (0, 0)),
            pl.BlockSpec((1, 1, tq, HEAD_W), lambda bi, hi, i: (bi, hi, i, 0)),
            pl.BlockSpec((1, 1, s, HEAD_W), lambda bi, hi, i: (bi, hi, 0, 0)),
            pl.BlockSpec((1, 1, nq, DV, tq), lambda bi, hi, i: (bi, hi, 0, 0, 0)),
        ],
        out_specs=pl.BlockSpec((1, tq, DV), lambda bi, hi, i: (bi, i, hi)),
        out_shape=jax.ShapeDtypeStruct((b, s, V_W), BF16),
        scratch_shapes=[
            pltpu.VMEM((1, 2 * tq), F32),
            pltpu.VMEM((1, 2 * tq), F32),
            pltpu.VMEM((DV, tq), F32),
            pltpu.VMEM((DV, tq), F32),
        ],
        compiler_params=pltpu.CompilerParams(
            dimension_semantics=("parallel", "parallel", "arbitrary"), vmem_limit_bytes=VMEM_LIMIT),
        name="p_attn",
    )(*lam_params, g_col, q, kb, vt)


def _post_math(x, mixb, mod, gains, w_out_ref, w_up_ref, w_down_ref):
    d = x.shape[-1]
    g_post_mix, g_pre_ffn, g_post_ffn = gains
    gt1 = mod[:, 2 * d:3 * d]
    sh2 = mod[:, 3 * d:4 * d]
    sc2 = mod[:, 4 * d:5 * d]
    gt2 = mod[:, 5 * d:6 * d]
    m = jnp.dot(mixb, w_out_ref[...], preferred_element_type=F32)
    x1 = x + gt1 * _rms(m, g_post_mix)
    h2 = _rms(x1, g_pre_ffn) * (1.0 + sc2) + sh2
    f = _mlp(h2.astype(BF16), w_up_ref, w_down_ref, w_up_ref.shape[1])
    return x1 + gt2 * _rms(f, g_post_ffn)


def _p_post_kernel(x_ref, o_ref, u_ref, uprev_ref, mod_ref, wdw_ref, bdw_ref, gcln_ref, bcln_ref,
                   gpm_ref, gpf_ref, gqf_ref, w_out_ref, w_up_ref, w_down_ref, y_ref, uext_ref, *, tm):
    i = pl.program_id(1)

    @pl.when(i == 0)
    def _():
        uext_ref[0:CONV_HALO, :] = jnp.zeros((CONV_HALO, CONV_CH), F32)

    @pl.when(i > 0)
    def _():
        uext_ref[0:CONV_HALO, :] = uprev_ref[0]

    uext_ref[CONV_HALO:, :] = u_ref[0]
    conv = jnp.zeros((tm, CONV_CH), F32) + bdw_ref[...]
    base = CONV_HALO - (CONV_K - 1)
    for j in range(CONV_K):
        conv = conv + wdw_ref[j:j + 1, :] * uext_ref[base + j:base + j + tm, :]
    cv = _conv_ln_swish(conv, gcln_ref[...], bcln_ref[...])
    mixb = jnp.concatenate([o_ref[0], cv.astype(BF16)], axis=-1)
    y_ref[0] = _post_math(x_ref[0], mixb, mod_ref[0], (gpm_ref[...], gpf_ref[...], gqf_ref[...]),
                          w_out_ref, w_up_ref, w_down_ref)


def _const_spec(shape, nargs):
    zeros = (0,) * len(shape)
    if nargs == 2:
        return pl.BlockSpec(shape, lambda a, b: zeros, pipeline_mode=pl.Buffered(1))
    return pl.BlockSpec(shape, lambda a: zeros, pipeline_mode=pl.Buffered(1))


def _p_post(x, o, u, mod3, w_dw, b_dw, g_cln, b_cln, g_post_mix, g_pre_ffn, g_post_ffn,
            w_out_b, w_up_b, w_down_b, tm):
    b, s, d = x.shape
    nt = s // tm
    d_ff = w_up_b.shape[1]
    halo_blocks = tm // CONV_HALO
    vec = lambda n: pl.BlockSpec((1, n), lambda bi, i: (0, 0))
    return pl.pallas_call(
        functools.partial(_p_post_kernel, tm=tm),
        grid=(b, nt),
        in_specs=[
            pl.BlockSpec((1, tm, d), lambda bi, i: (bi, i, 0)),
            pl.BlockSpec((1, tm, V_W), lambda bi, i: (bi, i, 0)),
            pl.BlockSpec((1, tm, CONV_CH), lambda bi, i: (bi, i, 0)),
            pl.BlockSpec((1, CONV_HALO, CONV_CH), lambda bi, i: (bi, jnp.maximum(i * halo_blocks - 1, 0), 0)),
            pl.BlockSpec((1, 1, mod3.shape[-1]), lambda bi, i: (bi, 0, 0)),
            pl.BlockSpec((CONV_K, CONV_CH), lambda bi, i: (0, 0)),
            vec(CONV_CH), vec(CONV_CH), vec(CONV_CH),
            vec(d), vec(d), vec(d),
            _const_spec((V_W + CONV_CH, d), 2),
            _const_spec((d, d_ff), 2),
            _const_spec((d_ff, d), 2),
        ],
        out_specs=pl.BlockSpec((1, tm, d), lambda bi, i: (bi, i, 0)),
        out_shape=jax.ShapeDtypeStruct((b, s, d), F32),
        scratch_shapes=[pltpu.VMEM((tm + CONV_HALO, CONV_CH), F32)],
        compiler_params=pltpu.CompilerParams(
            dimension_semantics=("parallel", "arbitrary"), vmem_limit_bytes=VMEM_LIMIT),
        name="p_post",
    )(x, o, u, u, mod3, w_dw, b_dw, g_cln, b_cln, g_post_mix, g_pre_ffn, g_post_ffn,
      w_out_b, w_up_b, w_down_b)


def _s_inproj_kernel(x_ref, mod_ref, g_ref, w_ref, cos_ref, sin_ref, q_ref, k_ref, v_ref, u_ref):
    d_model = x_ref.shape[-1]
    shift = mod_ref[:, 0:d_model]
    scale = mod_ref[:, d_model:2 * d_model]
    qs, ks, vs, u = _inproj_math(x_ref[...], shift, scale, g_ref[...], w_ref, cos_ref[...], sin_ref[...])
    for hh in range(N_HEADS):
        lo = hh * HEAD_W
        q_ref[:, lo:lo + HEAD_W] = qs[hh]
        k_ref[:, lo:lo + HEAD_W] = ks[hh]
        v_ref[:, lo:lo + DV] = vs[hh]
    u_ref[...] = u


def _s_inproj(x, mod, g_pre, w_in_b, cos_row, sin_row):
    rows, d = x.shape
    full = lambda a: pl.BlockSpec(a.shape, lambda i: (0,) * a.ndim)
    out = jax.ShapeDtypeStruct((rows, QK_W), F32)
    return pl.pallas_call(
        _s_inproj_kernel,
        grid=(1,),
        in_specs=[full(x), full(mod), full(g_pre), full(w_in_b), full(cos_row), full(sin_row)],
        out_specs=[pl.BlockSpec((rows, QK_W), lambda i: (0, 0))] * 4,
        out_shape=[out, out, out, out],
        compiler_params=pltpu.CompilerParams(vmem_limit_bytes=VMEM_LIMIT),
        name="s_inproj",
    )(x, mod, g_pre, w_in_b, cos_row, sin_row)


def _s_attn_kernel(pt_ref, lq1_ref, lk1_ref, lq2_ref, lk2_ref, q_ref, kn_ref, vn_ref, *rest,
                   lam_init, pps):
    k_refs = rest[:pps]
    v_refs = rest[pps:2 * pps]
    o_ref = rest[2 * pps]
    m_ref, l_ref, acc_ref = rest[2 * pps + 1:]
    step = pl.program_id(1)
    groups = 2 * N_HEADS

    q = q_ref[0]
    grp = lax.broadcasted_iota(jnp.int32, (groups, QK_W), 0)
    lane = lax.broadcasted_iota(jnp.int32, (groups, QK_W), 1)
    in_group = (lane >= grp * DK) & (lane < (grp + 1) * DK)
    qrows = jnp.where(in_group, jnp.broadcast_to(q, (groups, QK_W)), 0.0)

    @pl.when(step == 0)
    def _():
        m_ref[...] = jnp.sum(qrows * kn_ref[0], axis=-1, keepdims=True)
        l_ref[...] = jnp.ones(l_ref.shape, F32)
        acc_ref[...] = jnp.broadcast_to(vn_ref[0], acc_ref.shape)

    qb = qrows.astype(BF16)
    page = k_refs[0].shape[1] // N_HEADS

    def head_rows(refs, hh):
        return jnp.concatenate([r[0, pl.ds(hh, page, stride=N_HEADS), :] for r in refs], axis=0).astype(BF16)

    s = None
    for hh in range(N_HEADS):
        part = lax.dot_general(qb[:, hh * HEAD_W:(hh + 1) * HEAD_W], head_rows(k_refs, hh),
                               (((1,), (1,)), ((), ())), preferred_element_type=F32)
        s = part if s is None else s + part
    m_old = m_ref[...]
    m_new = jnp.maximum(m_old, jnp.max(s, axis=-1, keepdims=True))
    alpha = jnp.exp(m_old - m_new)
    p = jnp.exp(s - m_new)
    l_ref[...] = alpha * l_ref[...] + jnp.sum(p, axis=-1, keepdims=True)
    m_ref[...] = m_new
    pb = p.astype(BF16)
    pv = jnp.concatenate(
        [jnp.dot(pb, head_rows(v_refs, hh), preferred_element_type=F32) for hh in range(N_HEADS)], axis=-1)
    acc_ref[...] = acc_ref[...] * alpha + pv

    @pl.when(step == pl.num_programs(1) - 1)
    def _():
        lam = _lambda(lq1_ref[...], lk1_ref[...], lq2_ref[...], lk2_ref[...], lam_init)
        on = acc_ref[...] * (1.0 / l_ref[...])
        for hh in range(N_HEADS):
            lo = hh * DV
            o_ref[0, :, lo:lo + DV] = on[2 * hh:2 * hh + 1, lo:lo + DV] - lam * on[2 * hh + 1:2 * hh + 2, lo:lo + DV]


def _s_attn(page_table, lam_params, q, k_new, v_new, cache_k, cache_v, lam_init):
    db, n_pages = page_table.shape
    pps = math.gcd(PAGES_PER_STEP, n_pages)
    page_rows = cache_k.shape[1]
    q3, kn3, vn3 = (a.reshape(db, 1, a.shape[-1]) for a in (q, k_new, v_new))
    lam_spec = pl.BlockSpec((1, DK), lambda b, s, pt: (0, 0))
    row_spec = pl.BlockSpec((1, 1, QK_W), lambda b, s, pt: (b, 0, 0))

    def page_spec(i):
        return pl.BlockSpec((1, page_rows, HEAD_W), lambda b, s, pt: (pt[b, s * pps + i], 0, 0))

    grid_spec = pltpu.PrefetchScalarGridSpec(
        num_scalar_prefetch=1,
        grid=(db, n_pages // pps),
        in_specs=[lam_spec] * 4 + [row_spec] * 3 + [page_spec(i) for i in range(pps)] * 2,
        out_specs=pl.BlockSpec((1, 1, V_W), lambda b, s, pt: (b, 0, 0)),
        scratch_shapes=[
            pltpu.VMEM((2 * N_HEADS, 1), F32),
            pltpu.VMEM((2 * N_HEADS, 1), F32),
            pltpu.VMEM((2 * N_HEADS, V_W), F32),
        ],
    )
    out = pl.pallas_call(
        functools.partial(_s_attn_kernel, lam_init=lam_init, pps=pps),
        grid_spec=grid_spec,
        out_shape=jax.ShapeDtypeStruct((db, 1, V_W), F32),
        compiler_params=pltpu.CompilerParams(
            dimension_semantics=("parallel", "arbitrary"), vmem_limit_bytes=VMEM_LIMIT),
        name="s_attn",
    )(page_table, *lam_params, q3, kn3, vn3, *([cache_k] * pps), *([cache_v] * pps))
    return out.reshape(db, V_W)


def _s_post_kernel(x_ref, o_ref, u_ref, st_ref, mod_ref, gsub_ref, wdw_ref, bdw_ref, gcln_ref, bcln_ref,
                   gpm_ref, gpf_ref, gqf_ref, w_out_ref, w_up_ref, w_down_ref, y_ref, *, lam_init):
    o = o_ref[...]
    heads = []
    for hh in range(N_HEADS):
        oh = o[:, hh * DV:(hh + 1) * DV]
        heads.append(_rms(oh, gsub_ref[...]) * (1.0 - lam_init))
    conv = bdw_ref[...] + wdw_ref[CONV_K - 1:CONV_K, :] * u_ref[...]
    for j in range(CONV_K - 1):
        conv = conv + wdw_ref[j:j + 1, :] * st_ref[j]
    cv = _conv_ln_swish(conv, gcln_ref[...], bcln_ref[...])
    mixb = jnp.concatenate(heads + [cv], axis=-1).astype(BF16)
    y_ref[...] = _post_math(x_ref[...], mixb, mod_ref[...], (gpm_ref[...], gpf_ref[...], gqf_ref[...]),
                            w_out_ref, w_up_ref, w_down_ref)


def _s_post(x, o, u, state_t, mod, g_subln, w_dw, b_dw, g_cln, b_cln, g_post_mix, g_pre_ffn, g_post_ffn,
            w_out_b, w_up_b, w_down_b, lam_init):
    rows, d = x.shape
    args = (x, o, u, state_t, mod, g_subln, w_dw, b_dw, g_cln, b_cln, g_post_mix, g_pre_ffn, g_post_ffn,
            w_out_b, w_up_b, w_down_b)
    return pl.pallas_call(
        functools.partial(_s_post_kernel, lam_init=lam_init),
        grid=(1,),
        in_specs=[_const_spec(a.shape, 1) for a in args],
        out_specs=pl.BlockSpec((rows, d), lambda i: (0, 0)),
        out_shape=jax.ShapeDtypeStruct((rows, d), F32),
        compiler_params=pltpu.CompilerParams(vmem_limit_bytes=VMEM_LIMIT),
        name="s_post",
    )(*args)


def _rope_tables(pos):
    half = DK // 2
    inv_freq = ROPE_THETA ** (-jnp.arange(half, dtype=F32) / half)
    ang = pos.astype(F32)[:, None] * inv_freq[None, :]
    cos, sin = jnp.cos(ang), jnp.sin(ang)
    reps = LANES // DK
    cos128 = jnp.tile(jnp.concatenate([cos, cos], axis=-1), (1, reps))
    sin128 = jnp.tile(jnp.concatenate([-sin, sin], axis=-1), (1, reps))
    return cos128, sin128


def kernel(x_prompt, x_sample, c_prompt, c_sample, cache_k, cache_v, state_conv, page_table, w_in, w_out, lam_q1, lam_k1, lam_q2, lam_k2, g_subln, w_dw, b_dw, g_cln, b_cln, g_pre_mix, g_post_mix, g_pre_ffn, g_post_ffn, w_ada, b_ada, w_up, w_down):
    b, s, d = x_prompt.shape
    db, t, _ = x_sample.shape
    depth = w_in.shape[0]
    assert depth == 1 and t == 1, "kernel supports one layer and one new token per sample sequence"
    n_phys, page = cache_k.shape[1], cache_k.shape[2]
    past = page_table.shape[1] * page
    tm = min(ROW_TILE, s)
    assert s % tm == 0 and tm % CONV_HALO == 0
    lam_init = 0.8 - 0.6 * math.exp(-0.3 * 0)

    w_in_b = w_in[0].astype(BF16)
    w_out_b = w_out[0].astype(BF16)
    w_up_b = w_up[0].astype(BF16)
    w_down_b = w_down[0].astype(BF16)
    w_ada_b = w_ada[0].astype(BF16)
    row = lambda a: a.reshape(1, -1)
    lam_params = (row(lam_q1[0]), row(lam_k1[0]), row(lam_q2[0]), row(lam_k2[0]))
    g_pre, g_pm, g_pf, g_qf = row(g_pre_mix[0]), row(g_post_mix[0]), row(g_pre_ffn[0]), row(g_post_ffn[0])
    b_dw2, g_cln2, b_cln2 = row(b_dw[0]), row(g_cln[0]), row(b_cln[0])

    mod_p = _ada(c_prompt, w_ada_b, row(b_ada[0]))
    mod_s = _ada(c_sample, w_ada_b, row(b_ada[0]))
    mod_p3 = mod_p.reshape(b, 1, mod_p.shape[-1])

    cos_p, sin_p = _rope_tables(jnp.arange(s))
    q, kb, vt, k_p, v_p, u_p = _p_inproj(x_prompt, mod_p3, g_pre, w_in_b, cos_p, sin_p, tm)
    o_p = _p_attn(lam_params, g_subln[0].reshape(DV, 1), q, kb, vt, lam_init, tm)
    y_p = _p_post(x_prompt, o_p, u_p, mod_p3, w_dw[0], b_dw2, g_cln2, b_cln2, g_pm, g_pf, g_qf,
                  w_out_b, w_up_b, w_down_b, tm)

    cos_s, sin_s = _rope_tables(past + jnp.arange(t))
    xs = x_sample.reshape(db, d)
    q_s, k_s, v_s, u_s = _s_inproj(xs, mod_s, g_pre, w_in_b, cos_s, sin_s)
    ck = cache_k[0].reshape(n_phys, page * N_HEADS, HEAD_W)
    cv = cache_v[0].reshape(n_phys, page * N_HEADS, DV)
    o_s = _s_attn(page_table, lam_params, q_s, k_s, v_s, ck, cv, lam_init)
    state = state_conv[0]
    y_s = _s_post(xs, o_s, u_s, jnp.transpose(state, (1, 0, 2)), mod_s, row(g_subln[0]),
                  w_dw[0], b_dw2, g_cln2, b_cln2, g_pm, g_pf, g_qf, w_out_b, w_up_b, w_down_b, lam_init)

    k_prompt = k_p.reshape(1, b, s, N_HEADS, 2 * DK)
    v_prompt = v_p.reshape(1, b, s, N_HEADS, DV)
    conv_prompt = u_p[:, s - (CONV_K - 1):, :][None]
    k_sample = k_s.reshape(1, db, 1, N_HEADS, 2 * DK)
    v_sample = v_s.reshape(1, db, 1, N_HEADS, DV)
    conv_sample = jnp.concatenate([state[:, 1:, :], u_s[:, None, :]], axis=1)[None]
    return (y_p, y_s.reshape(db, 1, d), k_prompt, v_prompt, conv_prompt, k_sample, v_sample, conv_sample)
```

```python
import functools
import math

import jax
import jax.numpy as jnp
from jax import lax
from jax.experimental import pallas as pl
from jax.experimental.pallas import tpu as pltpu

F32 = jnp.float32
BF16 = jnp.bfloat16

N_HEADS = 4
DK = 64
DV = 2 * DK
HEAD_W = 2 * DK
QK_W = N_HEADS * 2 * DK
V_W = N_HEADS * DV
CONV_CH = 512
CONV_K = 31
ROPE_THETA = 10000.0
ATTN_SCALE = DK ** -0.5
LOG2_E = math.log2(math.e)
NORM_EPS = 1e-6
NEG_INF = -1e30

LANES = 128
SUBLANES = 8
ROW_TILE = 512
FF_CHUNK = 1024
ADA_COL_TILE = 1024
PAGES_PER_STEP = 8
SEQS_PER_STEP = 2
CONV_HALO = 32
VMEM_LIMIT = 56 * 1024 * 1024


def _rms(x, g):
    return x * lax.rsqrt(jnp.mean(x * x, axis=-1, keepdims=True) + NORM_EPS) * g


def _sigmoid(x):
    return 1.0 / (1.0 + jnp.exp(-x))


def _rope_128(x, cos, sin_signed):
    lane = lax.broadcasted_iota(jnp.int32, x.shape, 1)
    first_half = (lane & (DK - 1)) < (DK // 2)
    partner = jnp.where(first_half, pltpu.roll(x, LANES - DK // 2, 1), pltpu.roll(x, DK // 2, 1))
    return x * cos + partner * sin_signed


def _mlp(h2b, w_up_ref, w_down_ref, d_ff):
    f = None
    for c in range(d_ff // FF_CHUNK):
        hid = jnp.dot(h2b, w_up_ref[:, c * FF_CHUNK:(c + 1) * FF_CHUNK], preferred_element_type=F32)
        hid = jnp.maximum(hid, 0.0)
        hid = (hid * hid).astype(BF16)
        part = jnp.dot(hid, w_down_ref[c * FF_CHUNK:(c + 1) * FF_CHUNK, :], preferred_element_type=F32)
        f = part if f is None else f + part
    return f


def _conv_ln_swish(conv, g_cln, b_cln):
    mu = jnp.mean(conv, axis=-1, keepdims=True)
    cen = conv - mu
    var = jnp.mean(cen * cen, axis=-1, keepdims=True)
    y = cen * lax.rsqrt(var + NORM_EPS) * g_cln + b_cln
    return y * _sigmoid(y)


def _lambda(lq1, lk1, lq2, lk2, lam_init):
    s1 = jnp.sum(lq1 * lk1, axis=-1, keepdims=True)
    s2 = jnp.sum(lq2 * lk2, axis=-1, keepdims=True)
    return jnp.exp(s1) - jnp.exp(s2) + lam_init


def _ada_kernel(c_ref, w_ref, b_ref, o_ref):
    c = c_ref[...]
    s = (c * _sigmoid(c)).astype(BF16)
    o_ref[...] = jnp.dot(s, w_ref[...], preferred_element_type=F32) + b_ref[...]


def _ada(c, w_ada_b, b_ada):
    rows, d = c.shape
    n = w_ada_b.shape[1]
    return pl.pallas_call(
        _ada_kernel,
        grid=(n // ADA_COL_TILE,),
        in_specs=[
            pl.BlockSpec((rows, d), lambda j: (0, 0)),
            pl.BlockSpec((d, ADA_COL_TILE), lambda j: (0, j)),
            pl.BlockSpec((1, ADA_COL_TILE), lambda j: (0, j)),
        ],
        out_specs=pl.BlockSpec((rows, ADA_COL_TILE), lambda j: (0, j)),
        out_shape=jax.ShapeDtypeStruct((rows, n), F32),
        name="ada",
    )(c, w_ada_b, b_ada)


def _inproj_math(x, shift, scale, g_pre, w_in_ref, cos, sin_signed, q_scale):
    h = _rms(x, g_pre) * (1.0 + scale) + shift
    z = jnp.dot(h.astype(BF16), w_in_ref[...], preferred_element_type=F32)
    qs, ks, vs = [], [], []
    for hh in range(N_HEADS):
        lo = hh * HEAD_W
        qs.append(_rope_128(z[:, lo:lo + HEAD_W], cos, sin_signed) * q_scale)
        ks.append(_rope_128(z[:, QK_W + lo:QK_W + lo + HEAD_W], cos, sin_signed))
        vs.append(z[:, 2 * QK_W + lo:2 * QK_W + lo + DV])
    ua = z[:, 2 * QK_W + V_W:2 * QK_W + V_W + CONV_CH]
    ug = z[:, 2 * QK_W + V_W + CONV_CH:]
    u = ua * _sigmoid(ug)
    return qs, ks, vs, u


def _p_inproj_kernel(x_ref, mod_ref, g_ref, w_ref, cos_ref, sin_ref,
                     q_ref, kb_ref, vt_ref, k_ref, v_ref, u_ref):
    d_model = x_ref.shape[-1]
    tm = x_ref.shape[1]
    shift = mod_ref[0, :, 0:d_model]
    scale = mod_ref[0, :, d_model:2 * d_model]
    qs, ks, vs, u = _inproj_math(x_ref[0], shift, scale, g_ref[...], w_ref, cos_ref[...], sin_ref[...],
                                 ATTN_SCALE * LOG2_E)
    for hh in range(N_HEADS):
        q_ref[0, hh] = qs[hh].astype(BF16)
        kb_ref[0, hh] = ks[hh].astype(BF16)
        k_ref[0, pl.ds(hh, tm, stride=N_HEADS), :] = ks[hh]
        v_ref[0, pl.ds(hh, tm, stride=N_HEADS), :] = vs[hh]
        vt_ref[0, hh, 0] = vs[hh].T.astype(BF16)
    u_ref[0] = u


def _p_inproj(x, mod3, g_pre, w_in_b, cos, sin_signed, tm):
    b, s, d = x.shape
    nt = s // tm
    d_in = w_in_b.shape[1]
    return pl.pallas_call(
        _p_inproj_kernel,
        grid=(b, nt),
        in_specs=[
            pl.BlockSpec((1, tm, d), lambda bi, i: (bi, i, 0)),
            pl.BlockSpec((1, 1, mod3.shape[-1]), lambda bi, i: (bi, 0, 0)),
            pl.BlockSpec((1, d), lambda bi, i: (0, 0)),
            pl.BlockSpec((d, d_in), lambda bi, i: (0, 0)),
            pl.BlockSpec((tm, LANES), lambda bi, i: (i, 0)),
            pl.BlockSpec((tm, LANES), lambda bi, i: (i, 0)),
        ],
        out_specs=[
            pl.BlockSpec((1, N_HEADS, tm, HEAD_W), lambda bi, i: (bi, 0, i, 0)),
            pl.BlockSpec((1, N_HEADS, tm, HEAD_W), lambda bi, i: (bi, 0, i, 0)),
            pl.BlockSpec((1, N_HEADS, 1, DV, tm), lambda bi, i: (bi, 0, i, 0, 0)),
            pl.BlockSpec((1, tm * N_HEADS, HEAD_W), lambda bi, i: (bi, i, 0)),
            pl.BlockSpec((1, tm * N_HEADS, DV), lambda bi, i: (bi, i, 0)),
            pl.BlockSpec((1, tm, CONV_CH), lambda bi, i: (bi, i, 0)),
        ],
        out_shape=[
            jax.ShapeDtypeStruct((b, N_HEADS, s, HEAD_W), BF16),
            jax.ShapeDtypeStruct((b, N_HEADS, s, HEAD_W), BF16),
            jax.ShapeDtypeStruct((b, N_HEADS, nt, DV, tm), BF16),
            jax.ShapeDtypeStruct((b, s * N_HEADS, HEAD_W), F32),
            jax.ShapeDtypeStruct((b, s * N_HEADS, DV), F32),
            jax.ShapeDtypeStruct((b, s, CONV_CH), F32),
        ],
        compiler_params=pltpu.CompilerParams(
            dimension_semantics=("parallel", "arbitrary"), vmem_limit_bytes=VMEM_LIMIT),
        name="p_inproj",
    )(x, mod3, g_pre, w_in_b, cos, sin_signed)


def _p_attn_kernel(lq1_ref, lk1_ref, lq2_ref, lk2_ref, g_ref, q_ref, k_ref, vt_ref, o_ref,
                   m_ref, l_ref, acc0_ref, acc1_ref, sa_ref, sb_ref, *, lam_init, tq):
    qi = pl.program_id(2)
    q = q_ref[0, 0]
    lane = lax.broadcasted_iota(jnp.int32, q.shape, 1)
    zero = jnp.zeros_like(q)
    q2 = jnp.concatenate([jnp.where(lane < DK, q, zero), jnp.where(lane >= DK, q, zero)], axis=0)

    m_ref[...] = jnp.full(m_ref.shape, NEG_INF, F32)
    l_ref[...] = jnp.zeros(l_ref.shape, F32)
    acc0_ref[...] = jnp.zeros(acc0_ref.shape, F32)
    acc1_ref[...] = jnp.zeros(acc1_ref.shape, F32)

    def scores(c, dst_ref):
        start = pl.multiple_of(c * tq, tq)
        kc = k_ref[0, 0, pl.ds(start, tq), :]
        dst_ref[...] = lax.dot_general(kc, q2, (((1,), (1,)), ((), ())), preferred_element_type=F32)

    def softmax_pv(src_ref, c, masked):
        st = src_ref[...]
        vtc = vt_ref[0, 0, c]
        if masked:
            row = lax.broadcasted_iota(jnp.int32, st.shape, 0)
            col = lax.broadcasted_iota(jnp.int32, st.shape, 1)
            col = jnp.where(col >= tq, col - tq, col)
            st = jnp.where(row <= col, st, NEG_INF)
        m_old = m_ref[...]
        m_new = jnp.maximum(m_old, jnp.max(st, axis=0, keepdims=True))
        alpha = jnp.exp2(m_old - m_new)
        p = jnp.exp2(st - m_new)
        l_ref[...] = alpha * l_ref[...] + jnp.sum(p, axis=0, keepdims=True)
        m_ref[...] = m_new
        pb = p.astype(BF16)
        acc0_ref[...] = acc0_ref[...] * alpha[:, :tq] + jnp.dot(vtc, pb[:, :tq], preferred_element_type=F32)
        acc1_ref[...] = acc1_ref[...] * alpha[:, tq:] + jnp.dot(vtc, pb[:, tq:], preferred_element_type=F32)

    scores(0, sa_ref)

    def body(c, carry):
        @pl.when(c % 2 == 0)
        def _():
            scores(c + 1, sb_ref)
            softmax_pv(sa_ref, c, masked=False)

        @pl.when(c % 2 == 1)
        def _():
            scores(c + 1, sa_ref)
            softmax_pv(sb_ref, c, masked=False)

        return carry

    lax.fori_loop(0, qi, body, 0)

    @pl.when(qi % 2 == 0)
    def _():
        softmax_pv(sa_ref, qi, masked=True)

    @pl.when(qi % 2 == 1)
    def _():
        softmax_pv(sb_ref, qi, masked=True)

    lam = _lambda(lq1_ref[...], lk1_ref[...], lq2_ref[...], lk2_ref[...], lam_init)
    l = l_ref[...]
    ot = acc0_ref[...] * (1.0 / l[:, :tq]) - lam * (acc1_ref[...] * (1.0 / l[:, tq:]))
    ms = jnp.mean(ot * ot, axis=0, keepdims=True)
    ot = ot * lax.rsqrt(ms + NORM_EPS) * g_ref[...] * (1.0 - lam_init)
    o_ref[0] = ot.T.astype(BF16)


def _p_attn(lam_params, g_col, q, kb, vt, lam_init, tq):
    b, _, s, _ = q.shape
    nq = s // tq
    lam_spec = pl.BlockSpec((1, DK), lambda bi, hi, i: (0, 0))
    return pl.pallas_call(
        functools.partial(_p_attn_kernel, lam_init=lam_init, tq=tq),
        grid=(b, N_HEADS, nq),
        in_specs=[
            lam_spec, lam_spec, lam_spec, lam_spec,
            pl.BlockSpec((DV, 1), lambda bi, hi, i: (0, 0)),
            pl.BlockSpec((1, 1, tq, HEAD_W), lambda bi, hi, i: (bi, hi, i, 0)),
            pl.BlockSpec((1, 1, s, HEAD_W), lambda bi, hi, i: (bi, hi, 0, 0)),
            pl.BlockSpec((1, 1, nq, DV, tq), lambda bi, hi, i: (bi, hi, 0, 0, 0)),
        ],
        out_specs=pl.BlockSpec((1, tq, DV), lambda bi, hi, i: (bi, i, hi)),
        out_shape=jax.ShapeDtypeStruct((b, s, V_W), BF16),
        scratch_shapes=[
            pltpu.VMEM((1, 2 * tq), F32),
            pltpu.VMEM((1, 2 * tq), F32),
            pltpu.VMEM((DV, tq), F32),
            pltpu.VMEM((DV, tq), F32),
            pltpu.VMEM((tq, 2 * tq), F32),
            pltpu.VMEM((tq, 2 * tq), F32),
        ],
        compiler_params=pltpu.CompilerParams(
            dimension_semantics=("parallel", "parallel", "arbitrary"), vmem_limit_bytes=VMEM_LIMIT),
        name="p_attn",
    )(*lam_params, g_col, q, kb, vt)


def _post_math(x, mixb, mod, gains, w_out_ref, w_up_ref, w_down_ref):
    d = x.shape[-1]
    g_post_mix, g_pre_ffn, g_post_ffn = gains
    gt1 = mod[:, 2 * d:3 * d]
    sh2 = mod[:, 3 * d:4 * d]
    sc2 = mod[:, 4 * d:5 * d]
    gt2 = mod[:, 5 * d:6 * d]
    m = jnp.dot(mixb, w_out_ref[...], preferred_element_type=F32)
    x1 = x + gt1 * _rms(m, g_post_mix)
    h2 = _rms(x1, g_pre_ffn) * (1.0 + sc2) + sh2
    f = _mlp(h2.astype(BF16), w_up_ref, w_down_ref, w_up_ref.shape[1])
    return x1 + gt2 * _rms(f, g_post_ffn)


def _p_post_kernel(x_ref, o_ref, u_ref, uprev_ref, mod_ref, wdw_ref, bdw_ref, gcln_ref, bcln_ref,
                   gpm_ref, gpf_ref, gqf_ref, w_out_ref, w_up_ref, w_down_ref, y_ref, uext_ref, *, tm):
    i = pl.program_id(1)

    @pl.when(i == 0)
    def _():
        uext_ref[0:CONV_HALO, :] = jnp.zeros((CONV_HALO, CONV_CH), F32)

    @pl.when(i > 0)
    def _():
        uext_ref[0:CONV_HALO, :] = uprev_ref[0]

    uext_ref[CONV_HALO:, :] = u_ref[0]
    conv = jnp.zeros((tm, CONV_CH), F32) + bdw_ref[...]
    base = CONV_HALO - (CONV_K - 1)
    span = tm + CONV_HALO
    uext = uext_ref[...]
    for r in range(SUBLANES):
        taps = [j for j in range(CONV_K) if (base + j) % SUBLANES == r]
        if not taps:
            continue
        shifted = uext if r == 0 else pltpu.roll(uext, span - r, 0)
        for j in taps:
            lo = base + j - r
            conv = conv + wdw_ref[j:j + 1, :] * shifted[lo:lo + tm, :]
    cv = _conv_ln_swish(conv, gcln_ref[...], bcln_ref[...])
    mixb = jnp.concatenate([o_ref[0], cv.astype(BF16)], axis=-1)
    y_ref[0] = _post_math(x_ref[0], mixb, mod_ref[0], (gpm_ref[...], gpf_ref[...], gqf_ref[...]),
                          w_out_ref, w_up_ref, w_down_ref)


def _const_spec(shape, nargs):
    zeros = (0,) * len(shape)
    if nargs == 2:
        return pl.BlockSpec(shape, lambda a, b: zeros, pipeline_mode=pl.Buffered(1))
    return pl.BlockSpec(shape, lambda a: zeros, pipeline_mode=pl.Buffered(1))


def _p_post(x, o, u, mod3, w_dw, b_dw, g_cln, b_cln, g_post_mix, g_pre_ffn, g_post_ffn,
            w_out_b, w_up_b, w_down_b, tm):
    b, s, d = x.shape
    nt = s // tm
    d_ff = w_up_b.shape[1]
    halo_blocks = tm // CONV_HALO
    vec = lambda n: pl.BlockSpec((1, n), lambda bi, i: (0, 0))
    return pl.pallas_call(
        functools.partial(_p_post_kernel, tm=tm),
        grid=(b, nt),
        in_specs=[
            pl.BlockSpec((1, tm, d), lambda bi, i: (bi, i, 0)),
            pl.BlockSpec((1, tm, V_W), lambda bi, i: (bi, i, 0)),
            pl.BlockSpec((1, tm, CONV_CH), lambda bi, i: (bi, i, 0)),
            pl.BlockSpec((1, CONV_HALO, CONV_CH), lambda bi, i: (bi, jnp.maximum(i * halo_blocks - 1, 0), 0)),
            pl.BlockSpec((1, 1, mod3.shape[-1]), lambda bi, i: (bi, 0, 0)),
            pl.BlockSpec((CONV_K, CONV_CH), lambda bi, i: (0, 0)),
            vec(CONV_CH), vec(CONV_CH), vec(CONV_CH),
            vec(d), vec(d), vec(d),
            _const_spec((V_W + CONV_CH, d), 2),
            _const_spec((d, d_ff), 2),
            _const_spec((d_ff, d), 2),
        ],
        out_specs=pl.BlockSpec((1, tm, d), lambda bi, i: (bi, i, 0)),
        out_shape=jax.ShapeDtypeStruct((b, s, d), F32),
        scratch_shapes=[pltpu.VMEM((tm + CONV_HALO, CONV_CH), F32)],
        compiler_params=pltpu.CompilerParams(
            dimension_semantics=("parallel", "arbitrary"), vmem_limit_bytes=VMEM_LIMIT),
        name="p_post",
    )(x, o, u, u, mod3, w_dw, b_dw, g_cln, b_cln, g_post_mix, g_pre_ffn, g_post_ffn,
      w_out_b, w_up_b, w_down_b)


def _s_inproj_kernel(x_ref, mod_ref, g_ref, w_ref, cos_ref, sin_ref, q_ref, k_ref, v_ref, u_ref):
    d_model = x_ref.shape[-1]
    shift = mod_ref[:, 0:d_model]
    scale = mod_ref[:, d_model:2 * d_model]
    qs, ks, vs, u = _inproj_math(x_ref[...], shift, scale, g_ref[...], w_ref, cos_ref[...], sin_ref[...],
                                 ATTN_SCALE)
    for hh in range(N_HEADS):
        lo = hh * HEAD_W
        q_ref[:, lo:lo + HEAD_W] = qs[hh]
        k_ref[:, lo:lo + HEAD_W] = ks[hh]
        v_ref[:, lo:lo + DV] = vs[hh]
    u_ref[...] = u


def _s_inproj(x, mod, g_pre, w_in_b, cos_row, sin_row):
    rows, d = x.shape
    full = lambda a: pl.BlockSpec(a.shape, lambda i: (0,) * a.ndim)
    out = jax.ShapeDtypeStruct((rows, QK_W), F32)
    return pl.pallas_call(
        _s_inproj_kernel,
        grid=(1,),
        in_specs=[full(x), full(mod), full(g_pre), full(w_in_b), full(cos_row), full(sin_row)],
        out_specs=[pl.BlockSpec((rows, QK_W), lambda i: (0, 0))] * 4,
        out_shape=[out, out, out, out],
        compiler_params=pltpu.CompilerParams(vmem_limit_bytes=VMEM_LIMIT),
        name="s_inproj",
    )(x, mod, g_pre, w_in_b, cos_row, sin_row)


def _s_attn_kernel(pt_ref, lq1_ref, lk1_ref, lq2_ref, lk2_ref, q_ref, kn_ref, vn_ref, *rest,
                   lam_init, pps, spg):
    k_refs = rest[:spg * pps]
    v_refs = rest[spg * pps:2 * spg * pps]
    o_ref = rest[2 * spg * pps]
    m_ref, l_ref, acc_ref = rest[2 * spg * pps + 1:]
    step = pl.program_id(1)
    groups = 2 * N_HEADS
    page = k_refs[0].shape[1] // N_HEADS
    grp = lax.broadcasted_iota(jnp.int32, (groups, QK_W), 0)
    lane = lax.broadcasted_iota(jnp.int32, (groups, QK_W), 1)
    in_group = (lane >= grp * DK) & (lane < (grp + 1) * DK)

    def head_rows(refs, hh):
        return jnp.concatenate([r[0, pl.ds(hh, page, stride=N_HEADS), :] for r in refs], axis=0).astype(BF16)

    for r in range(spg):
        q = q_ref[0, r:r + 1, :]
        qrows = jnp.where(in_group, jnp.broadcast_to(q, (groups, QK_W)), 0.0)

        @pl.when(step == 0)
        def _():
            m_ref[r] = jnp.sum(qrows * kn_ref[0, r:r + 1, :], axis=-1, keepdims=True)
            l_ref[r] = jnp.ones(l_ref.shape[1:], F32)
            acc_ref[r] = jnp.broadcast_to(vn_ref[0, r:r + 1, :], acc_ref.shape[1:])

        qb = qrows.astype(BF16)
        kr = k_refs[r * pps:(r + 1) * pps]
        vr = v_refs[r * pps:(r + 1) * pps]
        s = None
        for hh in range(N_HEADS):
            part = lax.dot_general(qb[:, hh * HEAD_W:(hh + 1) * HEAD_W], head_rows(kr, hh),
                                   (((1,), (1,)), ((), ())), preferred_element_type=F32)
            s = part if s is None else s + part
        m_old = m_ref[r]
        m_new = jnp.maximum(m_old, jnp.max(s, axis=-1, keepdims=True))
        alpha = jnp.exp(m_old - m_new)
        p = jnp.exp(s - m_new)
        l_ref[r] = alpha * l_ref[r] + jnp.sum(p, axis=-1, keepdims=True)
        m_ref[r] = m_new
        pb = p.astype(BF16)
        pv = jnp.concatenate(
            [jnp.dot(pb, head_rows(vr, hh), preferred_element_type=F32) for hh in range(N_HEADS)], axis=-1)
        acc_ref[r] = acc_ref[r] * alpha + pv

    @pl.when(step == pl.num_programs(1) - 1)
    def _():
        lam = _lambda(lq1_ref[...], lk1_ref[...], lq2_ref[...], lk2_ref[...], lam_init)
        for r in range(spg):
            on = acc_ref[r] * (1.0 / l_ref[r])
            for hh in range(N_HEADS):
                lo = hh * DV
                o_ref[0, r:r + 1, lo:lo + DV] = (on[2 * hh:2 * hh + 1, lo:lo + DV]
                                                 - lam * on[2 * hh + 1:2 * hh + 2, lo:lo + DV])


def _s_attn(page_table, lam_params, q, k_new, v_new, cache_k, cache_v, lam_init):
    db, n_pages = page_table.shape
    pps = math.gcd(PAGES_PER_STEP, n_pages)
    spg = math.gcd(SEQS_PER_STEP, db)
    page_rows = cache_k.shape[1]
    q3, kn3, vn3 = (a.reshape(db // spg, spg, a.shape[-1]) for a in (q, k_new, v_new))
    lam_spec = pl.BlockSpec((1, DK), lambda b, s, pt: (0, 0))
    row_spec = pl.BlockSpec((1, spg, QK_W), lambda b, s, pt: (b, 0, 0))

    def page_spec(r, i):
        return pl.BlockSpec((1, page_rows, HEAD_W), lambda b, s, pt: (pt[b * spg + r, s * pps + i], 0, 0))

    page_specs = [page_spec(r, i) for r in range(spg) for i in range(pps)]
    grid_spec = pltpu.PrefetchScalarGridSpec(
        num_scalar_prefetch=1,
        grid=(db // spg, n_pages // pps),
        in_specs=[lam_spec] * 4 + [row_spec] * 3 + page_specs * 2,
        out_specs=pl.BlockSpec((1, spg, V_W), lambda b, s, pt: (b, 0, 0)),
        scratch_shapes=[
            pltpu.VMEM((spg, 2 * N_HEADS, 1), F32),
            pltpu.VMEM((spg, 2 * N_HEADS, 1), F32),
            pltpu.VMEM((spg, 2 * N_HEADS, V_W), F32),
        ],
    )
    n_page_args = spg * pps
    out = pl.pallas_call(
        functools.partial(_s_attn_kernel, lam_init=lam_init, pps=pps, spg=spg),
        grid_spec=grid_spec,
        out_shape=jax.ShapeDtypeStruct((db // spg, spg, V_W), F32),
        compiler_params=pltpu.CompilerParams(
            dimension_semantics=("parallel", "arbitrary"), vmem_limit_bytes=VMEM_LIMIT),
        name="s_attn",
    )(page_table, *lam_params, q3, kn3, vn3, *([cache_k] * n_page_args), *([cache_v] * n_page_args))
    return out.reshape(db, V_W)


def _s_post_kernel(x_ref, o_ref, u_ref, st_ref, mod_ref, gsub_ref, wdw_ref, bdw_ref, gcln_ref, bcln_ref,
                   gpm_ref, gpf_ref, gqf_ref, w_out_ref, w_up_ref, w_down_ref, y_ref, *, lam_init):
    o = o_ref[...]
    heads = []
    for hh in range(N_HEADS):
        oh = o[:, hh * DV:(hh + 1) * DV]
        heads.append(_rms(oh, gsub_ref[...]) * (1.0 - lam_init))
    conv = bdw_ref[...] + wdw_ref[CONV_K - 1:CONV_K, :] * u_ref[...]
    for j in range(CONV_K - 1):
        conv = conv + wdw_ref[j:j + 1, :] * st_ref[j]
    cv = _conv_ln_swish(conv, gcln_ref[...], bcln_ref[...])
    mixb = jnp.concatenate(heads + [cv], axis=-1).astype(BF16)
    y_ref[...] = _post_math(x_ref[...], mixb, mod_ref[...], (gpm_ref[...], gpf_ref[...], gqf_ref[...]),
                            w_out_ref, w_up_ref, w_down_ref)


def _s_post(x, o, u, state_t, mod, g_subln, w_dw, b_dw, g_cln, b_cln, g_post_mix, g_pre_ffn, g_post_ffn,
            w_out_b, w_up_b, w_down_b, lam_init):
    rows, d = x.shape
    args = (x, o, u, state_t, mod, g_subln, w_dw, b_dw, g_cln, b_cln, g_post_mix, g_pre_ffn, g_post_ffn,
            w_out_b, w_up_b, w_down_b)
    return pl.pallas_call(
        functools.partial(_s_post_kernel, lam_init=lam_init),
        grid=(1,),
        in_specs=[_const_spec(a.shape, 1) for a in args],
        out_specs=pl.BlockSpec((rows, d), lambda i: (0, 0)),
        out_shape=jax.ShapeDtypeStruct((rows, d), F32),
        compiler_params=pltpu.CompilerParams(vmem_limit_bytes=VMEM_LIMIT),
        name="s_post",
    )(*args)


def _rope_tables(pos):
    half = DK // 2
    inv_freq = ROPE_THETA ** (-jnp.arange(half, dtype=F32) / half)
    ang = pos.astype(F32)[:, None] * inv_freq[None, :]
    cos, sin = jnp.cos(ang), jnp.sin(ang)
    reps = LANES // DK
    cos128 = jnp.tile(jnp.concatenate([cos, cos], axis=-1), (1, reps))
    sin128 = jnp.tile(jnp.concatenate([-sin, sin], axis=-1), (1, reps))
    return cos128, sin128


def kernel(x_prompt, x_sample, c_prompt, c_sample, cache_k, cache_v, state_conv, page_table, w_in, w_out, lam_q1, lam_k1, lam_q2, lam_k2, g_subln, w_dw, b_dw, g_cln, b_cln, g_pre_mix, g_post_mix, g_pre_ffn, g_post_ffn, w_ada, b_ada, w_up, w_down):
    b, s, d = x_prompt.shape
    db, t, _ = x_sample.shape
    depth = w_in.shape[0]
    assert depth == 1 and t == 1, "kernel supports one layer and one new token per sample sequence"
    n_phys, page = cache_k.shape[1], cache_k.shape[2]
    past = page_table.shape[1] * page
    tm = min(ROW_TILE, s)
    assert s % tm == 0 and tm % CONV_HALO == 0
    lam_init = 0.8 - 0.6 * math.exp(-0.3 * 0)

    w_in_b = w_in[0].astype(BF16)
    w_out_b = w_out[0].astype(BF16)
    w_up_b = w_up[0].astype(BF16)
    w_down_b = w_down[0].astype(BF16)
    w_ada_b = w_ada[0].astype(BF16)
    row = lambda a: a.reshape(1, -1)
    lam_params = (row(lam_q1[0]), row(lam_k1[0]), row(lam_q2[0]), row(lam_k2[0]))
    g_pre, g_pm, g_pf, g_qf = row(g_pre_mix[0]), row(g_post_mix[0]), row(g_pre_ffn[0]), row(g_post_ffn[0])
    b_dw2, g_cln2, b_cln2 = row(b_dw[0]), row(g_cln[0]), row(b_cln[0])

    mod_p = _ada(c_prompt, w_ada_b, row(b_ada[0]))
    mod_s = _ada(c_sample, w_ada_b, row(b_ada[0]))
    mod_p3 = mod_p.reshape(b, 1, mod_p.shape[-1])

    cos_p, sin_p = _rope_tables(jnp.arange(s))
    q, kb, vt, k_p, v_p, u_p = _p_inproj(x_prompt, mod_p3, g_pre, w_in_b, cos_p, sin_p, tm)
    o_p = _p_attn(lam_params, g_subln[0].reshape(DV, 1), q, kb, vt, lam_init, tm)
    y_p = _p_post(x_prompt, o_p, u_p, mod_p3, w_dw[0], b_dw2, g_cln2, b_cln2, g_pm, g_pf, g_qf,
                  w_out_b, w_up_b, w_down_b, tm)

    cos_s, sin_s = _rope_tables(past + jnp.arange(t))
    xs = x_sample.reshape(db, d)
    q_s, k_s, v_s, u_s = _s_inproj(xs, mod_s, g_pre, w_in_b, cos_s, sin_s)
    ck = cache_k[0].reshape(n_phys, page * N_HEADS, HEAD_W)
    cv = cache_v[0].reshape(n_phys, page * N_HEADS, DV)
    o_s = _s_attn(page_table, lam_params, q_s, k_s, v_s, ck, cv, lam_init)
    state = state_conv[0]
    y_s = _s_post(xs, o_s, u_s, jnp.transpose(state, (1, 0, 2)), mod_s, row(g_subln[0]),
                  w_dw[0], b_dw2, g_cln2, b_cln2, g_pm, g_pf, g_qf, w_out_b, w_up_b, w_down_b, lam_init)

    k_prompt = k_p.reshape(1, b, s, N_HEADS, 2 * DK)
    v_prompt = v_p.reshape(1, b, s, N_HEADS, DV)
    conv_prompt = u_p[:, s - (CONV_K - 1):, :][None]
    k_sample = k_s.reshape(1, db, 1, N_HEADS, 2 * DK)
    v_sample = v_s.reshape(1, db, 1, N_HEADS, DV)
    conv_sample = jnp.concatenate([state[:, 1:, :], u_s[:, None, :]], axis=1)[None]
    return (y_p, y_s.reshape(db, 1, d), k_prompt, v_prompt, conv_prompt, k_sample, v_sample, conv_sample)
```

```python
import functools
import math

import jax
import jax.numpy as jnp
from jax import lax
from jax.experimental import pallas as pl
from jax.experimental.pallas import tpu as pltpu

F32 = jnp.float32
BF16 = jnp.bfloat16

N_HEADS = 4
DK = 64
DV = 2 * DK
HEAD_W = 2 * DK
QK_W = N_HEADS * 2 * DK
V_W = N_HEADS * DV
CONV_CH = 512
CONV_K = 31
ROPE_THETA = 10000.0
ATTN_SCALE = DK ** -0.5
LOG2_E = math.log2(math.e)
NORM_EPS = 1e-6
NEG_INF = -1e30

LANES = 128
SUBLANES = 8
ROW_TILE = 512
FF_CHUNK = 1024
ADA_COL_TILE = 1024
PAGES_PER_STEP = 8
SEQS_PER_STEP = 2
CONV_HALO = 32
VMEM_LIMIT = 56 * 1024 * 1024


def _rms(x, g):
    return x * lax.rsqrt(jnp.mean(x * x, axis=-1, keepdims=True) + NORM_EPS) * g


def _sigmoid(x):
    return 1.0 / (1.0 + jnp.exp(-x))


def _rope_128(x, cos, sin_signed):
    lane = lax.broadcasted_iota(jnp.int32, x.shape, 1)
    first_half = (lane & (DK - 1)) < (DK // 2)
    partner = jnp.where(first_half, pltpu.roll(x, LANES - DK // 2, 1), pltpu.roll(x, DK // 2, 1))
    return x * cos + partner * sin_signed


def _mlp(h2b, w_up_ref, w_down_ref, d_ff):
    f = None
    for c in range(d_ff // FF_CHUNK):
        hid = jnp.dot(h2b, w_up_ref[:, c * FF_CHUNK:(c + 1) * FF_CHUNK], preferred_element_type=F32)
        hid = jnp.maximum(hid, 0.0)
        hid = (hid * hid).astype(BF16)
        part = jnp.dot(hid, w_down_ref[c * FF_CHUNK:(c + 1) * FF_CHUNK, :], preferred_element_type=F32)
        f = part if f is None else f + part
    return f


def _causal_conv(uext, w_ref, bias, tm):
    span = tm + CONV_HALO
    base = CONV_HALO - (CONV_K - 1)
    conv = jnp.zeros((tm, CONV_CH), F32) + bias
    for r in range(SUBLANES):
        taps = [j for j in range(CONV_K) if (base + j) % SUBLANES == r]
        if not taps:
            continue
        shifted = uext if r == 0 else pltpu.roll(uext, span - r, 0)
        for j in taps:
            lo = base + j - r
            conv = conv + w_ref[j:j + 1, :] * shifted[lo:lo + tm, :]
    return conv


def _conv_ln_swish(conv, g_cln, b_cln):
    mu = jnp.mean(conv, axis=-1, keepdims=True)
    cen = conv - mu
    var = jnp.mean(cen * cen, axis=-1, keepdims=True)
    y = cen * lax.rsqrt(var + NORM_EPS) * g_cln + b_cln
    return y * _sigmoid(y)


def _lambda(lq1, lk1, lq2, lk2, lam_init):
    s1 = jnp.sum(lq1 * lk1, axis=-1, keepdims=True)
    s2 = jnp.sum(lq2 * lk2, axis=-1, keepdims=True)
    return jnp.exp(s1) - jnp.exp(s2) + lam_init


def _ada_kernel(c_ref, w_ref, b_ref, o_ref):
    c = c_ref[...]
    s = (c * _sigmoid(c)).astype(BF16)
    o_ref[...] = jnp.dot(s, w_ref[...], preferred_element_type=F32) + b_ref[...]


def _ada(c, w_ada_b, b_ada):
    rows, d = c.shape
    n = w_ada_b.shape[1]
    return pl.pallas_call(
        _ada_kernel,
        grid=(n // ADA_COL_TILE,),
        in_specs=[
            pl.BlockSpec((rows, d), lambda j: (0, 0)),
            pl.BlockSpec((d, ADA_COL_TILE), lambda j: (0, j)),
            pl.BlockSpec((1, ADA_COL_TILE), lambda j: (0, j)),
        ],
        out_specs=pl.BlockSpec((rows, ADA_COL_TILE), lambda j: (0, j)),
        out_shape=jax.ShapeDtypeStruct((rows, n), F32),
        name="ada",
    )(c, w_ada_b, b_ada)


def _inproj_math(x, shift, scale, g_pre, w_in_ref, cos, sin_signed, q_scale):
    h = _rms(x, g_pre) * (1.0 + scale) + shift
    z = jnp.dot(h.astype(BF16), w_in_ref[...], preferred_element_type=F32)
    qs, ks, vs = [], [], []
    for hh in range(N_HEADS):
        lo = hh * HEAD_W
        qs.append(_rope_128(z[:, lo:lo + HEAD_W], cos, sin_signed) * q_scale)
        ks.append(_rope_128(z[:, QK_W + lo:QK_W + lo + HEAD_W], cos, sin_signed))
        vs.append(z[:, 2 * QK_W + lo:2 * QK_W + lo + DV])
    ua = z[:, 2 * QK_W + V_W:2 * QK_W + V_W + CONV_CH]
    ug = z[:, 2 * QK_W + V_W + CONV_CH:]
    u = ua * _sigmoid(ug)
    return qs, ks, vs, u


def _p_inproj_kernel(x_ref, mod_ref, g_ref, w_ref, cos_ref, sin_ref,
                     q_ref, kb_ref, vt_ref, k_ref, v_ref, u_ref):
    d_model = x_ref.shape[-1]
    tm = x_ref.shape[1]
    shift = mod_ref[0, :, 0:d_model]
    scale = mod_ref[0, :, d_model:2 * d_model]
    qs, ks, vs, u = _inproj_math(x_ref[0], shift, scale, g_ref[...], w_ref, cos_ref[...], sin_ref[...],
                                 ATTN_SCALE * LOG2_E)
    for hh in range(N_HEADS):
        q_ref[0, hh] = qs[hh].astype(BF16)
        kb_ref[0, hh] = ks[hh].astype(BF16)
        k_ref[0, pl.ds(hh, tm, stride=N_HEADS), :] = ks[hh]
        v_ref[0, pl.ds(hh, tm, stride=N_HEADS), :] = vs[hh]
        vt_ref[0, hh, 0] = vs[hh].T.astype(BF16)
    u_ref[0] = u


def _p_inproj(x, mod3, g_pre, w_in_b, cos, sin_signed, tm):
    b, s, d = x.shape
    nt = s // tm
    d_in = w_in_b.shape[1]
    return pl.pallas_call(
        _p_inproj_kernel,
        grid=(b, nt),
        in_specs=[
            pl.BlockSpec((1, tm, d), lambda bi, i: (bi, i, 0)),
            pl.BlockSpec((1, 1, mod3.shape[-1]), lambda bi, i: (bi, 0, 0)),
            pl.BlockSpec((1, d), lambda bi, i: (0, 0)),
            pl.BlockSpec((d, d_in), lambda bi, i: (0, 0)),
            pl.BlockSpec((tm, LANES), lambda bi, i: (i, 0)),
            pl.BlockSpec((tm, LANES), lambda bi, i: (i, 0)),
        ],
        out_specs=[
            pl.BlockSpec((1, N_HEADS, tm, HEAD_W), lambda bi, i: (bi, 0, i, 0)),
            pl.BlockSpec((1, N_HEADS, tm, HEAD_W), lambda bi, i: (bi, 0, i, 0)),
            pl.BlockSpec((1, N_HEADS, 1, DV, tm), lambda bi, i: (bi, 0, i, 0, 0)),
            pl.BlockSpec((1, tm * N_HEADS, HEAD_W), lambda bi, i: (bi, i, 0)),
            pl.BlockSpec((1, tm * N_HEADS, DV), lambda bi, i: (bi, i, 0)),
            pl.BlockSpec((1, tm, CONV_CH), lambda bi, i: (bi, i, 0)),
        ],
        out_shape=[
            jax.ShapeDtypeStruct((b, N_HEADS, s, HEAD_W), BF16),
            jax.ShapeDtypeStruct((b, N_HEADS, s, HEAD_W), BF16),
            jax.ShapeDtypeStruct((b, N_HEADS, nt, DV, tm), BF16),
            jax.ShapeDtypeStruct((b, s * N_HEADS, HEAD_W), F32),
            jax.ShapeDtypeStruct((b, s * N_HEADS, DV), F32),
            jax.ShapeDtypeStruct((b, s, CONV_CH), F32),
        ],
        compiler_params=pltpu.CompilerParams(
            dimension_semantics=("parallel", "arbitrary"), vmem_limit_bytes=VMEM_LIMIT),
        name="p_inproj",
    )(x, mod3, g_pre, w_in_b, cos, sin_signed)


def _p_attn_kernel(lq1_ref, lk1_ref, lq2_ref, lk2_ref, g_ref, q_ref, k_ref, vt_ref, o_ref,
                   m_ref, l_ref, acc0_ref, acc1_ref, sa_ref, sb_ref, *, lam_init, tq):
    qi = pl.program_id(2)
    qt = q_ref[0, 0].astype(F32).T
    row = lax.broadcasted_iota(jnp.int32, qt.shape, 0)
    zero = jnp.zeros_like(qt)
    q2t = jnp.concatenate([jnp.where(row < DK, qt, zero), jnp.where(row >= DK, qt, zero)],
                          axis=1).astype(BF16)

    m_ref[...] = jnp.full(m_ref.shape, NEG_INF, F32)
    l_ref[...] = jnp.zeros(l_ref.shape, F32)
    acc0_ref[...] = jnp.zeros(acc0_ref.shape, F32)
    acc1_ref[...] = jnp.zeros(acc1_ref.shape, F32)

    def scores(c, dst_ref):
        start = pl.multiple_of(c * tq, tq)
        kc = k_ref[0, 0, pl.ds(start, tq), :]
        dst_ref[...] = jnp.dot(kc, q2t, preferred_element_type=F32)

    def softmax_pv(src_ref, c, masked):
        st = src_ref[...]
        vtc = vt_ref[0, 0, c]
        if masked:
            row = lax.broadcasted_iota(jnp.int32, st.shape, 0)
            col = lax.broadcasted_iota(jnp.int32, st.shape, 1)
            col = jnp.where(col >= tq, col - tq, col)
            st = jnp.where(row <= col, st, NEG_INF)
        m_old = m_ref[...]
        m_new = jnp.maximum(m_old, jnp.max(st, axis=0, keepdims=True))
        alpha = jnp.exp2(m_old - m_new)
        p = jnp.exp2(st - m_new)
        l_ref[...] = alpha * l_ref[...] + jnp.sum(p, axis=0, keepdims=True)
        m_ref[...] = m_new
        pb = p.astype(BF16)
        acc0_ref[...] = acc0_ref[...] * alpha[:, :tq] + jnp.dot(vtc, pb[:, :tq], preferred_element_type=F32)
        acc1_ref[...] = acc1_ref[...] * alpha[:, tq:] + jnp.dot(vtc, pb[:, tq:], preferred_element_type=F32)

    scores(0, sa_ref)

    def body(j, carry):
        c = 2 * j
        scores(c + 1, sb_ref)
        softmax_pv(sa_ref, c, masked=False)
        scores(c + 2, sa_ref)
        softmax_pv(sb_ref, c + 1, masked=False)
        return carry

    lax.fori_loop(0, qi // 2, body, 0)

    @pl.when(qi % 2 == 0)
    def _():
        softmax_pv(sa_ref, qi, masked=True)

    @pl.when(qi % 2 == 1)
    def _():
        scores(qi, sb_ref)
        softmax_pv(sa_ref, qi - 1, masked=False)
        softmax_pv(sb_ref, qi, masked=True)

    lam = _lambda(lq1_ref[...], lk1_ref[...], lq2_ref[...], lk2_ref[...], lam_init)
    l = l_ref[...]
    ot = acc0_ref[...] * (1.0 / l[:, :tq]) - lam * (acc1_ref[...] * (1.0 / l[:, tq:]))
    ms = jnp.mean(ot * ot, axis=0, keepdims=True)
    ot = ot * lax.rsqrt(ms + NORM_EPS) * g_ref[...] * (1.0 - lam_init)
    o_ref[0] = ot.T.astype(BF16)


def _p_attn(lam_params, g_col, q, kb, vt, lam_init, tq):
    b, _, s, _ = q.shape
    nq = s // tq
    lam_spec = pl.BlockSpec((1, DK), lambda bi, hi, i: (0, 0))
    return pl.pallas_call(
        functools.partial(_p_attn_kernel, lam_init=lam_init, tq=tq),
        grid=(b, N_HEADS, nq),
        in_specs=[
            lam_spec, lam_spec, lam_spec, lam_spec,
            pl.BlockSpec((DV, 1), lambda bi, hi, i: (0, 0)),
            pl.BlockSpec((1, 1, tq, HEAD_W), lambda bi, hi, i: (bi, hi, i, 0)),
            pl.BlockSpec((1, 1, s, HEAD_W), lambda bi, hi, i: (bi, hi, 0, 0)),
            pl.BlockSpec((1, 1, nq, DV, tq), lambda bi, hi, i: (bi, hi, 0, 0, 0)),
        ],
        out_specs=pl.BlockSpec((1, tq, DV), lambda bi, hi, i: (bi, i, hi)),
        out_shape=jax.ShapeDtypeStruct((b, s, V_W), BF16),
        scratch_shapes=[
            pltpu.VMEM((1, 2 * tq), F32),
            pltpu.VMEM((1, 2 * tq), F32),
            pltpu.VMEM((DV, tq), F32),
            pltpu.VMEM((DV, tq), F32),
            pltpu.VMEM((tq, 2 * tq), F32),
            pltpu.VMEM((tq, 2 * tq), F32),
        ],
        compiler_params=pltpu.CompilerParams(
            dimension_semantics=("parallel", "parallel", "arbitrary"), vmem_limit_bytes=VMEM_LIMIT),
        name="p_attn",
    )(*lam_params, g_col, q, kb, vt)


def _post_math(x, mixb, mod, gains, w_out_ref, w_up_ref, w_down_ref):
    d = x.shape[-1]
    g_post_mix, g_pre_ffn, g_post_ffn = gains
    gt1 = mod[:, 2 * d:3 * d]
    sh2 = mod[:, 3 * d:4 * d]
    sc2 = mod[:, 4 * d:5 * d]
    gt2 = mod[:, 5 * d:6 * d]
    m = jnp.dot(mixb, w_out_ref[...], preferred_element_type=F32)
    x1 = x + gt1 * _rms(m, g_post_mix)
    h2 = _rms(x1, g_pre_ffn) * (1.0 + sc2) + sh2
    f = _mlp(h2.astype(BF16), w_up_ref, w_down_ref, w_up_ref.shape[1])
    return x1 + gt2 * _rms(f, g_post_ffn)


def _p_post_kernel(x_ref, o_ref, u_ref, uprev_ref, mod_ref, wdw_ref, bdw_ref, gcln_ref, bcln_ref,
                   gpm_ref, gpf_ref, gqf_ref, w_out_ref, w_up_ref, w_down_ref, y_ref, uext_ref, *, tm):
    i = pl.program_id(1)

    @pl.when(i == 0)
    def _():
        uext_ref[0:CONV_HALO, :] = jnp.zeros((CONV_HALO, CONV_CH), F32)

    @pl.when(i > 0)
    def _():
        uext_ref[0:CONV_HALO, :] = uprev_ref[0]

    uext_ref[CONV_HALO:, :] = u_ref[0]
    conv = _causal_conv(uext_ref[...], wdw_ref, bdw_ref[...], tm)
    cv = _conv_ln_swish(conv, gcln_ref[...], bcln_ref[...])
    mixb = jnp.concatenate([o_ref[0], cv.astype(BF16)], axis=-1)
    y_ref[0] = _post_math(x_ref[0], mixb, mod_ref[0], (gpm_ref[...], gpf_ref[...], gqf_ref[...]),
                          w_out_ref, w_up_ref, w_down_ref)


def _const_spec(shape, nargs):
    zeros = (0,) * len(shape)
    if nargs == 2:
        return pl.BlockSpec(shape, lambda a, b: zeros, pipeline_mode=pl.Buffered(1))
    return pl.BlockSpec(shape, lambda a: zeros, pipeline_mode=pl.Buffered(1))


def _p_post(x, o, u, mod3, w_dw, b_dw, g_cln, b_cln, g_post_mix, g_pre_ffn, g_post_ffn,
            w_out_b, w_up_b, w_down_b, tm):
    b, s, d = x.shape
    nt = s // tm
    d_ff = w_up_b.shape[1]
    halo_blocks = tm // CONV_HALO
    vec = lambda n: pl.BlockSpec((1, n), lambda bi, i: (0, 0))
    return pl.pallas_call(
        functools.partial(_p_post_kernel, tm=tm),
        grid=(b, nt),
        in_specs=[
            pl.BlockSpec((1, tm, d), lambda bi, i: (bi, i, 0)),
            pl.BlockSpec((1, tm, V_W), lambda bi, i: (bi, i, 0)),
            pl.BlockSpec((1, tm, CONV_CH), lambda bi, i: (bi, i, 0)),
            pl.BlockSpec((1, CONV_HALO, CONV_CH), lambda bi, i: (bi, jnp.maximum(i * halo_blocks - 1, 0), 0)),
            pl.BlockSpec((1, 1, mod3.shape[-1]), lambda bi, i: (bi, 0, 0)),
            pl.BlockSpec((CONV_K, CONV_CH), lambda bi, i: (0, 0)),
            vec(CONV_CH), vec(CONV_CH), vec(CONV_CH),
            vec(d), vec(d), vec(d),
            _const_spec((V_W + CONV_CH, d), 2),
            _const_spec((d, d_ff), 2),
            _const_spec((d_ff, d), 2),
        ],
        out_specs=pl.BlockSpec((1, tm, d), lambda bi, i: (bi, i, 0)),
        out_shape=jax.ShapeDtypeStruct((b, s, d), F32),
        scratch_shapes=[pltpu.VMEM((tm + CONV_HALO, CONV_CH), F32)],
        compiler_params=pltpu.CompilerParams(
            dimension_semantics=("parallel", "arbitrary"), vmem_limit_bytes=VMEM_LIMIT),
        name="p_post",
    )(x, o, u, u, mod3, w_dw, b_dw, g_cln, b_cln, g_post_mix, g_pre_ffn, g_post_ffn,
      w_out_b, w_up_b, w_down_b)


def _s_inproj_kernel(x_ref, mod_ref, g_ref, w_ref, cos_ref, sin_ref, q_ref, k_ref, v_ref, u_ref):
    d_model = x_ref.shape[-1]
    shift = mod_ref[:, 0:d_model]
    scale = mod_ref[:, d_model:2 * d_model]
    qs, ks, vs, u = _inproj_math(x_ref[...], shift, scale, g_ref[...], w_ref, cos_ref[...], sin_ref[...],
                                 ATTN_SCALE)
    for hh in range(N_HEADS):
        lo = hh * HEAD_W
        q_ref[:, lo:lo + HEAD_W] = qs[hh]
        k_ref[:, lo:lo + HEAD_W] = ks[hh]
        v_ref[:, lo:lo + DV] = vs[hh]
    u_ref[...] = u


def _s_inproj(x, mod, g_pre, w_in_b, cos_row, sin_row):
    rows, d = x.shape
    full = lambda a: pl.BlockSpec(a.shape, lambda i: (0,) * a.ndim)
    out = jax.ShapeDtypeStruct((rows, QK_W), F32)
    return pl.pallas_call(
        _s_inproj_kernel,
        grid=(1,),
        in_specs=[full(x), full(mod), full(g_pre), full(w_in_b), full(cos_row), full(sin_row)],
        out_specs=[pl.BlockSpec((rows, QK_W), lambda i: (0, 0))] * 4,
        out_shape=[out, out, out, out],
        compiler_params=pltpu.CompilerParams(vmem_limit_bytes=VMEM_LIMIT),
        name="s_inproj",
    )(x, mod, g_pre, w_in_b, cos_row, sin_row)


def _s_attn_kernel(pt_ref, lq1_ref, lk1_ref, lq2_ref, lk2_ref, q_ref, kn_ref, vn_ref, *rest,
                   lam_init, pps, spg):
    k_refs = rest[:spg * pps]
    v_refs = rest[spg * pps:2 * spg * pps]
    o_ref = rest[2 * spg * pps]
    m_ref, l_ref, acc_ref = rest[2 * spg * pps + 1:]
    step = pl.program_id(1)
    groups = 2 * N_HEADS
    page = k_refs[0].shape[1] // N_HEADS
    grp = lax.broadcasted_iota(jnp.int32, (groups, QK_W), 0)
    lane = lax.broadcasted_iota(jnp.int32, (groups, QK_W), 1)
    in_group = (lane >= grp * DK) & (lane < (grp + 1) * DK)

    def head_rows(refs, hh):
        return jnp.concatenate([r[0, pl.ds(hh, page, stride=N_HEADS), :] for r in refs], axis=0).astype(BF16)

    for r in range(spg):
        q = q_ref[0, r:r + 1, :]
        qrows = jnp.where(in_group, jnp.broadcast_to(q, (groups, QK_W)), 0.0)

        @pl.when(step == 0)
        def _():
            m_ref[r] = jnp.sum(qrows * kn_ref[0, r:r + 1, :], axis=-1, keepdims=True)
            l_ref[r] = jnp.ones(l_ref.shape[1:], F32)
            acc_ref[r] = jnp.broadcast_to(vn_ref[0, r:r + 1, :], acc_ref.shape[1:])

        qb = qrows.astype(BF16)
        kr = k_refs[r * pps:(r + 1) * pps]
        vr = v_refs[r * pps:(r + 1) * pps]
        s = None
        for hh in range(N_HEADS):
            part = lax.dot_general(qb[:, hh * HEAD_W:(hh + 1) * HEAD_W], head_rows(kr, hh),
                                   (((1,), (1,)), ((), ())), preferred_element_type=F32)
            s = part if s is None else s + part
        m_old = m_ref[r]
        m_new = jnp.maximum(m_old, jnp.max(s, axis=-1, keepdims=True))
        alpha = jnp.exp(m_old - m_new)
        p = jnp.exp(s - m_new)
        l_ref[r] = alpha * l_ref[r] + jnp.sum(p, axis=-1, keepdims=True)
        m_ref[r] = m_new
        pb = p.astype(BF16)
        pv = jnp.concatenate(
            [jnp.dot(pb, head_rows(vr, hh), preferred_element_type=F32) for hh in range(N_HEADS)], axis=-1)
        acc_ref[r] = acc_ref[r] * alpha + pv

    @pl.when(step == pl.num_programs(1) - 1)
    def _():
        lam = _lambda(lq1_ref[...], lk1_ref[...], lq2_ref[...], lk2_ref[...], lam_init)
        for r in range(spg):
            on = acc_ref[r] * (1.0 / l_ref[r])
            for hh in range(N_HEADS):
                lo = hh * DV
                o_ref[0, r:r + 1, lo:lo + DV] = (on[2 * hh:2 * hh + 1, lo:lo + DV]
                                                 - lam * on[2 * hh + 1:2 * hh + 2, lo:lo + DV])


def _s_attn(page_table, lam_params, q, k_new, v_new, cache_k, cache_v, lam_init):
    db, n_pages = page_table.shape
    pps = math.gcd(PAGES_PER_STEP, n_pages)
    spg = math.gcd(SEQS_PER_STEP, db)
    page_rows = cache_k.shape[1]
    q3, kn3, vn3 = (a.reshape(db // spg, spg, a.shape[-1]) for a in (q, k_new, v_new))
    lam_spec = pl.BlockSpec((1, DK), lambda b, s, pt: (0, 0))
    row_spec = pl.BlockSpec((1, spg, QK_W), lambda b, s, pt: (b, 0, 0))

    def page_spec(r, i):
        return pl.BlockSpec((1, page_rows, HEAD_W), lambda b, s, pt: (pt[b * spg + r, s * pps + i], 0, 0))

    page_specs = [page_spec(r, i) for r in range(spg) for i in range(pps)]
    grid_spec = pltpu.PrefetchScalarGridSpec(
        num_scalar_prefetch=1,
        grid=(db // spg, n_pages // pps),
        in_specs=[lam_spec] * 4 + [row_spec] * 3 + page_specs * 2,
        out_specs=pl.BlockSpec((1, spg, V_W), lambda b, s, pt: (b, 0, 0)),
        scratch_shapes=[
            pltpu.VMEM((spg, 2 * N_HEADS, 1), F32),
            pltpu.VMEM((spg, 2 * N_HEADS, 1), F32),
            pltpu.VMEM((spg, 2 * N_HEADS, V_W), F32),
        ],
    )
    n_page_args = spg * pps
    out = pl.pallas_call(
        functools.partial(_s_attn_kernel, lam_init=lam_init, pps=pps, spg=spg),
        grid_spec=grid_spec,
        out_shape=jax.ShapeDtypeStruct((db // spg, spg, V_W), F32),
        compiler_params=pltpu.CompilerParams(
            dimension_semantics=("parallel", "arbitrary"), vmem_limit_bytes=VMEM_LIMIT),
        name="s_attn",
    )(page_table, *lam_params, q3, kn3, vn3, *([cache_k] * n_page_args), *([cache_v] * n_page_args))
    return out.reshape(db, V_W)


def _s_post_kernel(x_ref, o_ref, u_ref, st_ref, mod_ref, gsub_ref, wdw_ref, bdw_ref, gcln_ref, bcln_ref,
                   gpm_ref, gpf_ref, gqf_ref, w_out_ref, w_up_ref, w_down_ref, y_ref, *, lam_init):
    o = o_ref[...]
    heads = []
    for hh in range(N_HEADS):
        oh = o[:, hh * DV:(hh + 1) * DV]
        heads.append(_rms(oh, gsub_ref[...]) * (1.0 - lam_init))
    conv = bdw_ref[...] + wdw_ref[CONV_K - 1:CONV_K, :] * u_ref[...]
    for j in range(CONV_K - 1):
        conv = conv + wdw_ref[j:j + 1, :] * st_ref[j]
    cv = _conv_ln_swish(conv, gcln_ref[...], bcln_ref[...])
    mixb = jnp.concatenate(heads + [cv], axis=-1).astype(BF16)
    y_ref[...] = _post_math(x_ref[...], mixb, mod_ref[...], (gpm_ref[...], gpf_ref[...], gqf_ref[...]),
                            w_out_ref, w_up_ref, w_down_ref)


def _s_post(x, o, u, state_t, mod, g_subln, w_dw, b_dw, g_cln, b_cln, g_post_mix, g_pre_ffn, g_post_ffn,
            w_out_b, w_up_b, w_down_b, lam_init):
    rows, d = x.shape
    args = (x, o, u, state_t, mod, g_subln, w_dw, b_dw, g_cln, b_cln, g_post_mix, g_pre_ffn, g_post_ffn,
            w_out_b, w_up_b, w_down_b)
    return pl.pallas_call(
        functools.partial(_s_post_kernel, lam_init=lam_init),
        grid=(1,),
        in_specs=[_const_spec(a.shape, 1) for a in args],
        out_specs=pl.BlockSpec((rows, d), lambda i: (0, 0)),
        out_shape=jax.ShapeDtypeStruct((rows, d), F32),
        compiler_params=pltpu.CompilerParams(vmem_limit_bytes=VMEM_LIMIT),
        name="s_post",
    )(*args)


def _rope_tables(pos):
    half = DK // 2
    inv_freq = ROPE_THETA ** (-jnp.arange(half, dtype=F32) / half)
    ang = pos.astype(F32)[:, None] * inv_freq[None, :]
    cos, sin = jnp.cos(ang), jnp.sin(ang)
    reps = LANES // DK
    cos128 = jnp.tile(jnp.concatenate([cos, cos], axis=-1), (1, reps))
    sin128 = jnp.tile(jnp.concatenate([-sin, sin], axis=-1), (1, reps))
    return cos128, sin128


def kernel(x_prompt, x_sample, c_prompt, c_sample, cache_k, cache_v, state_conv, page_table, w_in, w_out, lam_q1, lam_k1, lam_q2, lam_k2, g_subln, w_dw, b_dw, g_cln, b_cln, g_pre_mix, g_post_mix, g_pre_ffn, g_post_ffn, w_ada, b_ada, w_up, w_down):
    b, s, d = x_prompt.shape
    db, t, _ = x_sample.shape
    depth = w_in.shape[0]
    assert depth == 1 and t == 1, "kernel supports one layer and one new token per sample sequence"
    n_phys, page = cache_k.shape[1], cache_k.shape[2]
    past = page_table.shape[1] * page
    tm = min(ROW_TILE, s)
    assert s % tm == 0 and tm % CONV_HALO == 0
    lam_init = 0.8 - 0.6 * math.exp(-0.3 * 0)

    w_in_b = w_in[0].astype(BF16)
    w_out_b = w_out[0].astype(BF16)
    w_up_b = w_up[0].astype(BF16)
    w_down_b = w_down[0].astype(BF16)
    w_ada_b = w_ada[0].astype(BF16)
    row = lambda a: a.reshape(1, -1)
    lam_params = (row(lam_q1[0]), row(lam_k1[0]), row(lam_q2[0]), row(lam_k2[0]))
    g_pre, g_pm, g_pf, g_qf = row(g_pre_mix[0]), row(g_post_mix[0]), row(g_pre_ffn[0]), row(g_post_ffn[0])
    b_dw2, g_cln2, b_cln2 = row(b_dw[0]), row(g_cln[0]), row(b_cln[0])

    mod_p = _ada(c_prompt, w_ada_b, row(b_ada[0]))
    mod_s = _ada(c_sample, w_ada_b, row(b_ada[0]))
    mod_p3 = mod_p.reshape(b, 1, mod_p.shape[-1])

    cos_p, sin_p = _rope_tables(jnp.arange(s))
    q, kb, vt, k_p, v_p, u_p = _p_inproj(x_prompt, mod_p3, g_pre, w_in_b, cos_p, sin_p, tm)
    o_p = _p_attn(lam_params, g_subln[0].reshape(DV, 1), q, kb, vt, lam_init, tm)
    y_p = _p_post(x_prompt, o_p, u_p, mod_p3, w_dw[0], b_dw2, g_cln2, b_cln2, g_pm, g_pf, g_qf,
                  w_out_b, w_up_b, w_down_b, tm)

    cos_s, sin_s = _rope_tables(past + jnp.arange(t))
    xs = x_sample.reshape(db, d)
    q_s, k_s, v_s, u_s = _s_inproj(xs, mod_s, g_pre, w_in_b, cos_s, sin_s)
    ck = cache_k[0].reshape(n_phys, page * N_HEADS, HEAD_W)
    cv = cache_v[0].reshape(n_phys, page * N_HEADS, DV)
    o_s = _s_attn(page_table, lam_params, q_s, k_s, v_s, ck, cv, lam_init)
    state = state_conv[0]
    y_s = _s_post(xs, o_s, u_s, jnp.transpose(state, (1, 0, 2)), mod_s, row(g_subln[0]),
                  w_dw[0], b_dw2, g_cln2, b_cln2, g_pm, g_pf, g_qf, w_out_b, w_up_b, w_down_b, lam_init)

    k_prompt = k_p.reshape(1, b, s, N_HEADS, 2 * DK)
    v_prompt = v_p.reshape(1, b, s, N_HEADS, DV)
    conv_prompt = u_p[:, s - (CONV_K - 1):, :][None]
    k_sample = k_s.reshape(1, db, 1, N_HEADS, 2 * DK)
    v_sample = v_s.reshape(1, db, 1, N_HEADS, DV)
    conv_sample = jnp.concatenate([state[:, 1:, :], u_s[:, None, :]], axis=1)[None]
    return (y_p, y_s.reshape(db, 1, d), k_prompt, v_prompt, conv_prompt, k_sample, v_sample, conv_sample)
```

```python
import functools
import math

import jax
import jax.numpy as jnp
from jax import lax
from jax.experimental import pallas as pl
from jax.experimental.pallas import tpu as pltpu

F32 = jnp.float32
BF16 = jnp.bfloat16

N_HEADS = 4
DK = 64
DV = 2 * DK
HEAD_W = 2 * DK
QK_W = N_HEADS * 2 * DK
V_W = N_HEADS * DV
CONV_CH = 512
CONV_K = 31
ROPE_THETA = 10000.0
ATTN_SCALE = DK ** -0.5
LOG2_E = math.log2(math.e)
NORM_EPS = 1e-6
NEG_INF = -1e30

LANES = 128
SUBLANES = 8
ROW_TILE = 512
FF_CHUNK = 1024
ATTN_UNROLL = 4
SCORE_PAD = LANES
ADA_COL_TILE = 1024
PAGES_PER_STEP = 8
SEQS_PER_STEP = 2
CONV_HALO = 32
VMEM_LIMIT = 56 * 1024 * 1024


def _rms(x, g):
    return x * lax.rsqrt(jnp.mean(x * x, axis=-1, keepdims=True) + NORM_EPS) * g


def _sigmoid(x):
    return 1.0 / (1.0 + jnp.exp(-x))


def _rope_128(x, cos, sin_signed):
    lane = lax.broadcasted_iota(jnp.int32, x.shape, 1)
    first_half = (lane & (DK - 1)) < (DK // 2)
    partner = jnp.where(first_half, pltpu.roll(x, LANES - DK // 2, 1), pltpu.roll(x, DK // 2, 1))
    return x * cos + partner * sin_signed


def _mlp(h2b, w_up_ref, w_down_ref, d_ff):
    f = None
    for c in range(d_ff // FF_CHUNK):
        hid = jnp.dot(h2b, w_up_ref[:, c * FF_CHUNK:(c + 1) * FF_CHUNK], preferred_element_type=F32)
        hid = jnp.maximum(hid, 0.0)
        hid = (hid * hid).astype(BF16)
        part = jnp.dot(hid, w_down_ref[c * FF_CHUNK:(c + 1) * FF_CHUNK, :], preferred_element_type=F32)
        f = part if f is None else f + part
    return f


def _causal_conv(uext, w_ref, bias, tm):
    span = tm + CONV_HALO
    base = CONV_HALO - (CONV_K - 1)
    conv = jnp.zeros((tm, CONV_CH), F32) + bias
    for r in range(SUBLANES):
        taps = [j for j in range(CONV_K) if (base + j) % SUBLANES == r]
        if not taps:
            continue
        shifted = uext if r == 0 else pltpu.roll(uext, span - r, 0)
        for j in taps:
            lo = base + j - r
            conv = conv + w_ref[j:j + 1, :] * shifted[lo:lo + tm, :]
    return conv


def _conv_ln_swish(conv, g_cln, b_cln):
    mu = jnp.mean(conv, axis=-1, keepdims=True)
    cen = conv - mu
    var = jnp.mean(cen * cen, axis=-1, keepdims=True)
    y = cen * lax.rsqrt(var + NORM_EPS) * g_cln + b_cln
    return y * _sigmoid(y)


def _lambda(lq1, lk1, lq2, lk2, lam_init):
    s1 = jnp.sum(lq1 * lk1, axis=-1, keepdims=True)
    s2 = jnp.sum(lq2 * lk2, axis=-1, keepdims=True)
    return jnp.exp(s1) - jnp.exp(s2) + lam_init


def _ada_kernel(c_ref, w_ref, b_ref, o_ref):
    c = c_ref[...]
    s = (c * _sigmoid(c)).astype(BF16)
    o_ref[...] = jnp.dot(s, w_ref[...], preferred_element_type=F32) + b_ref[...]


def _ada(c, w_ada_b, b_ada):
    rows, d = c.shape
    n = w_ada_b.shape[1]
    return pl.pallas_call(
        _ada_kernel,
        grid=(n // ADA_COL_TILE,),
        in_specs=[
            pl.BlockSpec((rows, d), lambda j: (0, 0)),
            pl.BlockSpec((d, ADA_COL_TILE), lambda j: (0, j)),
            pl.BlockSpec((1, ADA_COL_TILE), lambda j: (0, j)),
        ],
        out_specs=pl.BlockSpec((rows, ADA_COL_TILE), lambda j: (0, j)),
        out_shape=jax.ShapeDtypeStruct((rows, n), F32),
        name="ada",
    )(c, w_ada_b, b_ada)


def _inproj_math(x, shift, scale, g_pre, w_in_ref, cos, sin_signed, q_scale):
    h = _rms(x, g_pre) * (1.0 + scale) + shift
    z = jnp.dot(h.astype(BF16), w_in_ref[...], preferred_element_type=F32)
    qs, ks, vs = [], [], []
    for hh in range(N_HEADS):
        lo = hh * HEAD_W
        qs.append(_rope_128(z[:, lo:lo + HEAD_W], cos, sin_signed) * q_scale)
        ks.append(_rope_128(z[:, QK_W + lo:QK_W + lo + HEAD_W], cos, sin_signed))
        vs.append(z[:, 2 * QK_W + lo:2 * QK_W + lo + DV])
    ua = z[:, 2 * QK_W + V_W:2 * QK_W + V_W + CONV_CH]
    ug = z[:, 2 * QK_W + V_W + CONV_CH:]
    u = ua * _sigmoid(ug)
    return qs, ks, vs, u


def _p_inproj_kernel(x_ref, mod_ref, g_ref, w_ref, cos_ref, sin_ref,
                     q_ref, kb_ref, vt_ref, k_ref, v_ref, u_ref):
    d_model = x_ref.shape[-1]
    tm = x_ref.shape[1]
    shift = mod_ref[0, :, 0:d_model]
    scale = mod_ref[0, :, d_model:2 * d_model]
    qs, ks, vs, u = _inproj_math(x_ref[0], shift, scale, g_ref[...], w_ref, cos_ref[...], sin_ref[...],
                                 ATTN_SCALE * LOG2_E)
    for hh in range(N_HEADS):
        q_ref[0, hh] = qs[hh].astype(BF16)
        kb_ref[0, hh] = ks[hh].astype(BF16)
        k_ref[0, pl.ds(hh, tm, stride=N_HEADS), :] = ks[hh]
        v_ref[0, pl.ds(hh, tm, stride=N_HEADS), :] = vs[hh]
        vt_ref[0, hh, 0] = vs[hh].T.astype(BF16)
    u_ref[0] = u


def _p_inproj(x, mod3, g_pre, w_in_b, cos, sin_signed, tm):
    b, s, d = x.shape
    nt = s // tm
    d_in = w_in_b.shape[1]
    return pl.pallas_call(
        _p_inproj_kernel,
        grid=(b, nt),
        in_specs=[
            pl.BlockSpec((1, tm, d), lambda bi, i: (bi, i, 0)),
            pl.BlockSpec((1, 1, mod3.shape[-1]), lambda bi, i: (bi, 0, 0)),
            pl.BlockSpec((1, d), lambda bi, i: (0, 0)),
            pl.BlockSpec((d, d_in), lambda bi, i: (0, 0)),
            pl.BlockSpec((tm, LANES), lambda bi, i: (i, 0)),
            pl.BlockSpec((tm, LANES), lambda bi, i: (i, 0)),
        ],
        out_specs=[
            pl.BlockSpec((1, N_HEADS, tm, HEAD_W), lambda bi, i: (bi, 0, i, 0)),
            pl.BlockSpec((1, N_HEADS, tm, HEAD_W), lambda bi, i: (bi, 0, i, 0)),
            pl.BlockSpec((1, N_HEADS, 1, DV, tm), lambda bi, i: (bi, 0, i, 0, 0)),
            pl.BlockSpec((1, tm * N_HEADS, HEAD_W), lambda bi, i: (bi, i, 0)),
            pl.BlockSpec((1, tm * N_HEADS, DV), lambda bi, i: (bi, i, 0)),
            pl.BlockSpec((1, tm, CONV_CH), lambda bi, i: (bi, i, 0)),
        ],
        out_shape=[
            jax.ShapeDtypeStruct((b, N_HEADS, s, HEAD_W), BF16),
            jax.ShapeDtypeStruct((b, N_HEADS, s, HEAD_W), BF16),
            jax.ShapeDtypeStruct((b, N_HEADS, nt, DV, tm), BF16),
            jax.ShapeDtypeStruct((b, s * N_HEADS, HEAD_W), F32),
            jax.ShapeDtypeStruct((b, s * N_HEADS, DV), F32),
            jax.ShapeDtypeStruct((b, s, CONV_CH), F32),
        ],
        compiler_params=pltpu.CompilerParams(
            dimension_semantics=("parallel", "arbitrary"), vmem_limit_bytes=VMEM_LIMIT),
        name="p_inproj",
    )(x, mod3, g_pre, w_in_b, cos, sin_signed)


def _p_attn_kernel(lq1_ref, lk1_ref, lq2_ref, lk2_ref, g_ref, q_ref, k_ref, vt_ref, o_ref,
                   m_ref, l_ref, acc0_ref, acc1_ref, sa_ref, sb_ref, *, lam_init, tq):
    qi = pl.program_id(2)
    qt = q_ref[0, 0].astype(F32).T
    row = lax.broadcasted_iota(jnp.int32, qt.shape, 0)
    zero = jnp.zeros_like(qt)
    q2t = jnp.concatenate([jnp.where(row < DK, qt, zero), jnp.where(row >= DK, qt, zero)],
                          axis=1).astype(BF16)

    m_ref[...] = jnp.full(m_ref.shape, NEG_INF, F32)
    l_ref[...] = jnp.zeros(l_ref.shape, F32)
    acc0_ref[...] = jnp.zeros(acc0_ref.shape, F32)
    acc1_ref[...] = jnp.zeros(acc1_ref.shape, F32)

    def scores(c, dst_ref):
        start = pl.multiple_of(c * tq, tq)
        kc = k_ref[0, 0, pl.ds(start, tq), :]
        dst_ref[:, 0:2 * tq] = jnp.dot(kc, q2t, preferred_element_type=F32)

    def softmax_pv(src_ref, c, masked):
        st = src_ref[:, 0:2 * tq]
        vtc = vt_ref[0, 0, c]
        if masked:
            row = lax.broadcasted_iota(jnp.int32, st.shape, 0)
            col = lax.broadcasted_iota(jnp.int32, st.shape, 1)
            col = jnp.where(col >= tq, col - tq, col)
            st = jnp.where(row <= col, st, NEG_INF)
        m_old = m_ref[...]
        m_new = jnp.maximum(m_old, jnp.max(st, axis=0, keepdims=True))
        alpha = jnp.exp2(m_old - m_new)
        p = jnp.exp2(st - m_new)
        l_ref[...] = alpha * l_ref[...] + jnp.sum(p, axis=0, keepdims=True)
        m_ref[...] = m_new
        pb = p.astype(BF16)
        acc0_ref[...] = acc0_ref[...] * alpha[:, :tq] + jnp.dot(vtc, pb[:, :tq], preferred_element_type=F32)
        acc1_ref[...] = acc1_ref[...] * alpha[:, tq:] + jnp.dot(vtc, pb[:, tq:], preferred_element_type=F32)

    bufs = (sa_ref, sb_ref)

    def run(c0, n, masked_last, prefetch_after):
        for t in range(n):
            if t + 1 < n or prefetch_after:
                scores(c0 + t + 1, bufs[(t + 1) % 2])
            softmax_pv(bufs[t % 2], c0 + t, masked=masked_last and t == n - 1)

    scores(0, sa_ref)

    def body(j, carry):
        run(ATTN_UNROLL * j, ATTN_UNROLL, masked_last=False, prefetch_after=True)
        return carry

    lax.fori_loop(0, qi // ATTN_UNROLL, body, 0)

    for r in range(ATTN_UNROLL):
        @pl.when(qi % ATTN_UNROLL == r)
        def _():
            run(qi - r, r + 1, masked_last=True, prefetch_after=False)

    lam = _lambda(lq1_ref[...], lk1_ref[...], lq2_ref[...], lk2_ref[...], lam_init)
    l = l_ref[...]
    ot = acc0_ref[...] * (1.0 / l[:, :tq]) - lam * (acc1_ref[...] * (1.0 / l[:, tq:]))
    ms = jnp.mean(ot * ot, axis=0, keepdims=True)
    ot = ot * lax.rsqrt(ms + NORM_EPS) * g_ref[...] * (1.0 - lam_init)
    o_ref[0] = ot.T.astype(BF16)


def _p_attn(lam_params, g_col, q, kb, vt, lam_init, tq):
    b, _, s, _ = q.shape
    nq = s // tq
    lam_spec = pl.BlockSpec((1, DK), lambda bi, hi, i: (0, 0))
    return pl.pallas_call(
        functools.partial(_p_attn_kernel, lam_init=lam_init, tq=tq),
        grid=(b, N_HEADS, nq),
        in_specs=[
            lam_spec, lam_spec, lam_spec, lam_spec,
            pl.BlockSpec((DV, 1), lambda bi, hi, i: (0, 0)),
            pl.BlockSpec((1, 1, tq, HEAD_W), lambda bi, hi, i: (bi, hi, i, 0)),
            pl.BlockSpec((1, 1, s, HEAD_W), lambda bi, hi, i: (bi, hi, 0, 0)),
            pl.BlockSpec((1, 1, nq, DV, tq), lambda bi, hi, i: (bi, hi, 0, 0, 0)),
        ],
        out_specs=pl.BlockSpec((1, tq, DV), lambda bi, hi, i: (bi, i, hi)),
        out_shape=jax.ShapeDtypeStruct((b, s, V_W), BF16),
        scratch_shapes=[
            pltpu.VMEM((1, 2 * tq), F32),
            pltpu.VMEM((1, 2 * tq), F32),
            pltpu.VMEM((DV, tq), F32),
            pltpu.VMEM((DV, tq), F32),
            pltpu.VMEM((tq, 2 * tq + SCORE_PAD), F32),
            pltpu.VMEM((tq, 2 * tq + SCORE_PAD), F32),
        ],
        compiler_params=pltpu.CompilerParams(
            dimension_semantics=("parallel", "parallel", "arbitrary"), vmem_limit_bytes=VMEM_LIMIT),
        name="p_attn",
    )(*lam_params, g_col, q, kb, vt)


def _post_math(x, mixb, mod, gains, w_out_ref, w_up_ref, w_down_ref):
    d = x.shape[-1]
    g_post_mix, g_pre_ffn, g_post_ffn = gains
    gt1 = mod[:, 2 * d:3 * d]
    sh2 = mod[:, 3 * d:4 * d]
    sc2 = mod[:, 4 * d:5 * d]
    gt2 = mod[:, 5 * d:6 * d]
    m = jnp.dot(mixb, w_out_ref[...], preferred_element_type=F32)
    x1 = x + gt1 * _rms(m, g_post_mix)
    h2 = _rms(x1, g_pre_ffn) * (1.0 + sc2) + sh2
    f = _mlp(h2.astype(BF16), w_up_ref, w_down_ref, w_up_ref.shape[1])
    return x1 + gt2 * _rms(f, g_post_ffn)


def _p_post_kernel(x_ref, o_ref, u_ref, uprev_ref, mod_ref, wdw_ref, bdw_ref, gcln_ref, bcln_ref,
                   gpm_ref, gpf_ref, gqf_ref, w_out_ref, w_up_ref, w_down_ref, y_ref, uext_ref, *, tm):
    i = pl.program_id(1)

    @pl.when(i == 0)
    def _():
        uext_ref[0:CONV_HALO, :] = jnp.zeros((CONV_HALO, CONV_CH), F32)

    @pl.when(i > 0)
    def _():
        uext_ref[0:CONV_HALO, :] = uprev_ref[0]

    uext_ref[CONV_HALO:, :] = u_ref[0]
    conv = _causal_conv(uext_ref[...], wdw_ref, bdw_ref[...], tm)
    cv = _conv_ln_swish(conv, gcln_ref[...], bcln_ref[...])
    mixb = jnp.concatenate([o_ref[0], cv.astype(BF16)], axis=-1)
    y_ref[0] = _post_math(x_ref[0], mixb, mod_ref[0], (gpm_ref[...], gpf_ref[...], gqf_ref[...]),
                          w_out_ref, w_up_ref, w_down_ref)


def _const_spec(shape, nargs):
    zeros = (0,) * len(shape)
    if nargs == 2:
        return pl.BlockSpec(shape, lambda a, b: zeros, pipeline_mode=pl.Buffered(1))
    return pl.BlockSpec(shape, lambda a: zeros, pipeline_mode=pl.Buffered(1))


def _p_post(x, o, u, mod3, w_dw, b_dw, g_cln, b_cln, g_post_mix, g_pre_ffn, g_post_ffn,
            w_out_b, w_up_b, w_down_b, tm):
    b, s, d = x.shape
    nt = s // tm
    d_ff = w_up_b.shape[1]
    halo_blocks = tm // CONV_HALO
    vec = lambda n: pl.BlockSpec((1, n), lambda bi, i: (0, 0))
    return pl.pallas_call(
        functools.partial(_p_post_kernel, tm=tm),
        grid=(b, nt),
        in_specs=[
            pl.BlockSpec((1, tm, d), lambda bi, i: (bi, i, 0)),
            pl.BlockSpec((1, tm, V_W), lambda bi, i: (bi, i, 0)),
            pl.BlockSpec((1, tm, CONV_CH), lambda bi, i: (bi, i, 0)),
            pl.BlockSpec((1, CONV_HALO, CONV_CH), lambda bi, i: (bi, jnp.maximum(i * halo_blocks - 1, 0), 0)),
            pl.BlockSpec((1, 1, mod3.shape[-1]), lambda bi, i: (bi, 0, 0)),
            pl.BlockSpec((CONV_K, CONV_CH), lambda bi, i: (0, 0)),
            vec(CONV_CH), vec(CONV_CH), vec(CONV_CH),
            vec(d), vec(d), vec(d),
            _const_spec((V_W + CONV_CH, d), 2),
            _const_spec((d, d_ff), 2),
            _const_spec((d_ff, d), 2),
        ],
        out_specs=pl.BlockSpec((1, tm, d), lambda bi, i: (bi, i, 0)),
        out_shape=jax.ShapeDtypeStruct((b, s, d), F32),
        scratch_shapes=[pltpu.VMEM((tm + CONV_HALO, CONV_CH), F32)],
        compiler_params=pltpu.CompilerParams(
            dimension_semantics=("parallel", "arbitrary"), vmem_limit_bytes=VMEM_LIMIT),
        name="p_post",
    )(x, o, u, u, mod3, w_dw, b_dw, g_cln, b_cln, g_post_mix, g_pre_ffn, g_post_ffn,
      w_out_b, w_up_b, w_down_b)


def _s_inproj_kernel(x_ref, mod_ref, g_ref, w_ref, cos_ref, sin_ref, q_ref, k_ref, v_ref, u_ref):
    d_model = x_ref.shape[-1]
    shift = mod_ref[:, 0:d_model]
    scale = mod_ref[:, d_model:2 * d_model]
    qs, ks, vs, u = _inproj_math(x_ref[...], shift, scale, g_ref[...], w_ref, cos_ref[...], sin_ref[...],
                                 ATTN_SCALE)
    for hh in range(N_HEADS):
        lo = hh * HEAD_W
        q_ref[:, lo:lo + HEAD_W] = qs[hh]
        k_ref[:, lo:lo + HEAD_W] = ks[hh]
        v_ref[:, lo:lo + DV] = vs[hh]
    u_ref[...] = u


def _s_inproj(x, mod, g_pre, w_in_b, cos_row, sin_row):
    rows, d = x.shape
    full = lambda a: pl.BlockSpec(a.shape, lambda i: (0,) * a.ndim)
    out = jax.ShapeDtypeStruct((rows, QK_W), F32)
    return pl.pallas_call(
        _s_inproj_kernel,
        grid=(1,),
        in_specs=[full(x), full(mod), full(g_pre), full(w_in_b), full(cos_row), full(sin_row)],
        out_specs=[pl.BlockSpec((rows, QK_W), lambda i: (0, 0))] * 4,
        out_shape=[out, out, out, out],
        compiler_params=pltpu.CompilerParams(vmem_limit_bytes=VMEM_LIMIT),
        name="s_inproj",
    )(x, mod, g_pre, w_in_b, cos_row, sin_row)


def _s_attn_kernel(pt_ref, lq1_ref, lk1_ref, lq2_ref, lk2_ref, q_ref, kn_ref, vn_ref, *rest,
                   lam_init, pps, spg):
    k_refs = rest[:spg * pps]
    v_refs = rest[spg * pps:2 * spg * pps]
    o_ref = rest[2 * spg * pps]
    m_ref, l_ref, acc_ref = rest[2 * spg * pps + 1:]
    step = pl.program_id(1)
    groups = 2 * N_HEADS
    page = k_refs[0].shape[1] // N_HEADS
    grp = lax.broadcasted_iota(jnp.int32, (groups, QK_W), 0)
    lane = lax.broadcasted_iota(jnp.int32, (groups, QK_W), 1)
    in_group = (lane >= grp * DK) & (lane < (grp + 1) * DK)

    def head_rows(refs, hh):
        return jnp.concatenate([r[0, pl.ds(hh, page, stride=N_HEADS), :] for r in refs], axis=0).astype(BF16)

    for r in range(spg):
        q = q_ref[0, r:r + 1, :]
        qrows = jnp.where(in_group, jnp.broadcast_to(q, (groups, QK_W)), 0.0)

        @pl.when(step == 0)
        def _():
            m_ref[r] = jnp.sum(qrows * kn_ref[0, r:r + 1, :], axis=-1, keepdims=True)
            l_ref[r] = jnp.ones(l_ref.shape[1:], F32)
            acc_ref[r] = jnp.broadcast_to(vn_ref[0, r:r + 1, :], acc_ref.shape[1:])

        qb = qrows.astype(BF16)
        kr = k_refs[r * pps:(r + 1) * pps]
        vr = v_refs[r * pps:(r + 1) * pps]
        s = None
        for hh in range(N_HEADS):
            part = lax.dot_general(qb[:, hh * HEAD_W:(hh + 1) * HEAD_W], head_rows(kr, hh),
                                   (((1,), (1,)), ((), ())), preferred_element_type=F32)
            s = part if s is None else s + part
        m_old = m_ref[r]
        m_new = jnp.maximum(m_old, jnp.max(s, axis=-1, keepdims=True))
        alpha = jnp.exp(m_old - m_new)
        p = jnp.exp(s - m_new)
        l_ref[r] = alpha * l_ref[r] + jnp.sum(p, axis=-1, keepdims=True)
        m_ref[r] = m_new
        pb = p.astype(BF16)
        pv = jnp.concatenate(
            [jnp.dot(pb, head_rows(vr, hh), preferred_element_type=F32) for hh in range(N_HEADS)], axis=-1)
        acc_ref[r] = acc_ref[r] * alpha + pv

    @pl.when(step == pl.num_programs(1) - 1)
    def _():
        lam = _lambda(lq1_ref[...], lk1_ref[...], lq2_ref[...], lk2_ref[...], lam_init)
        for r in range(spg):
            on = acc_ref[r] * (1.0 / l_ref[r])
            for hh in range(N_HEADS):
                lo = hh * DV
                o_ref[0, r:r + 1, lo:lo + DV] = (on[2 * hh:2 * hh + 1, lo:lo + DV]
                                                 - lam * on[2 * hh + 1:2 * hh + 2, lo:lo + DV])


def _s_attn(page_table, lam_params, q, k_new, v_new, cache_k, cache_v, lam_init):
    db, n_pages = page_table.shape
    pps = math.gcd(PAGES_PER_STEP, n_pages)
    spg = math.gcd(SEQS_PER_STEP, db)
    page_rows = cache_k.shape[1]
    q3, kn3, vn3 = (a.reshape(db // spg, spg, a.shape[-1]) for a in (q, k_new, v_new))
    lam_spec = pl.BlockSpec((1, DK), lambda b, s, pt: (0, 0))
    row_spec = pl.BlockSpec((1, spg, QK_W), lambda b, s, pt: (b, 0, 0))

    def page_spec(r, i):
        return pl.BlockSpec((1, page_rows, HEAD_W), lambda b, s, pt: (pt[b * spg + r, s * pps + i], 0, 0))

    page_specs = [page_spec(r, i) for r in range(spg) for i in range(pps)]
    grid_spec = pltpu.PrefetchScalarGridSpec(
        num_scalar_prefetch=1,
        grid=(db // spg, n_pages // pps),
        in_specs=[lam_spec] * 4 + [row_spec] * 3 + page_specs * 2,
        out_specs=pl.BlockSpec((1, spg, V_W), lambda b, s, pt: (b, 0, 0)),
        scratch_shapes=[
            pltpu.VMEM((spg, 2 * N_HEADS, 1), F32),
            pltpu.VMEM((spg, 2 * N_HEADS, 1), F32),
            pltpu.VMEM((spg, 2 * N_HEADS, V_W), F32),
        ],
    )
    n_page_args = spg * pps
    out = pl.pallas_call(
        functools.partial(_s_attn_kernel, lam_init=lam_init, pps=pps, spg=spg),
        grid_spec=grid_spec,
        out_shape=jax.ShapeDtypeStruct((db // spg, spg, V_W), F32),
        compiler_params=pltpu.CompilerParams(
            dimension_semantics=("parallel", "arbitrary"), vmem_limit_bytes=VMEM_LIMIT),
        name="s_attn",
    )(page_table, *lam_params, q3, kn3, vn3, *([cache_k] * n_page_args), *([cache_v] * n_page_args))
    return out.reshape(db, V_W)


def _s_post_kernel(x_ref, o_ref, u_ref, st_ref, mod_ref, gsub_ref, wdw_ref, bdw_ref, gcln_ref, bcln_ref,
                   gpm_ref, gpf_ref, gqf_ref, w_out_ref, w_up_ref, w_down_ref, y_ref, *, lam_init):
    o = o_ref[...]
    heads = []
    for hh in range(N_HEADS):
        oh = o[:, hh * DV:(hh + 1) * DV]
        heads.append(_rms(oh, gsub_ref[...]) * (1.0 - lam_init))
    conv = bdw_ref[...] + wdw_ref[CONV_K - 1:CONV_K, :] * u_ref[...]
    for j in range(CONV_K - 1):
        conv = conv + wdw_ref[j:j + 1, :] * st_ref[j]
    cv = _conv_ln_swish(conv, gcln_ref[...], bcln_ref[...])
    mixb = jnp.concatenate(heads + [cv], axis=-1).astype(BF16)
    y_ref[...] = _post_math(x_ref[...], mixb, mod_ref[...], (gpm_ref[...], gpf_ref[...], gqf_ref[...]),
                            w_out_ref, w_up_ref, w_down_ref)


def _s_post(x, o, u, state_t, mod, g_subln, w_dw, b_dw, g_cln, b_cln, g_post_mix, g_pre_ffn, g_post_ffn,
            w_out_b, w_up_b, w_down_b, lam_init):
    rows, d = x.shape
    args = (x, o, u, state_t, mod, g_subln, w_dw, b_dw, g_cln, b_cln, g_post_mix, g_pre_ffn, g_post_ffn,
            w_out_b, w_up_b, w_down_b)
    return pl.pallas_call(
        functools.partial(_s_post_kernel, lam_init=lam_init),
        grid=(1,),
        in_specs=[_const_spec(a.shape, 1) for a in args],
        out_specs=pl.BlockSpec((rows, d), lambda i: (0, 0)),
        out_shape=jax.ShapeDtypeStruct((rows, d), F32),
        compiler_params=pltpu.CompilerParams(vmem_limit_bytes=VMEM_LIMIT),
        name="s_post",
    )(*args)


def _rope_tables(pos):
    half = DK // 2
    inv_freq = ROPE_THETA ** (-jnp.arange(half, dtype=F32) / half)
    ang = pos.astype(F32)[:, None] * inv_freq[None, :]
    cos, sin = jnp.cos(ang), jnp.sin(ang)
    reps = LANES // DK
    cos128 = jnp.tile(jnp.concatenate([cos, cos], axis=-1), (1, reps))
    sin128 = jnp.tile(jnp.concatenate([-sin, sin], axis=-1), (1, reps))
    return cos128, sin128


def kernel(x_prompt, x_sample, c_prompt, c_sample, cache_k, cache_v, state_conv, page_table, w_in, w_out, lam_q1, lam_k1, lam_q2, lam_k2, g_subln, w_dw, b_dw, g_cln, b_cln, g_pre_mix, g_post_mix, g_pre_ffn, g_post_ffn, w_ada, b_ada, w_up, w_down):
    b, s, d = x_prompt.shape
    db, t, _ = x_sample.shape
    depth = w_in.shape[0]
    assert depth == 1 and t == 1, "kernel supports one layer and one new token per sample sequence"
    n_phys, page = cache_k.shape[1], cache_k.shape[2]
    past = page_table.shape[1] * page
    tm = min(ROW_TILE, s)
    assert s % tm == 0 and tm % CONV_HALO == 0
    lam_init = 0.8 - 0.6 * math.exp(-0.3 * 0)

    w_in_b = w_in[0].astype(BF16)
    w_out_b = w_out[0].astype(BF16)
    w_up_b = w_up[0].astype(BF16)
    w_down_b = w_down[0].astype(BF16)
    w_ada_b = w_ada[0].astype(BF16)
    row = lambda a: a.reshape(1, -1)
    lam_params = (row(lam_q1[0]), row(lam_k1[0]), row(lam_q2[0]), row(lam_k2[0]))
    g_pre, g_pm, g_pf, g_qf = row(g_pre_mix[0]), row(g_post_mix[0]), row(g_pre_ffn[0]), row(g_post_ffn[0])
    b_dw2, g_cln2, b_cln2 = row(b_dw[0]), row(g_cln[0]), row(b_cln[0])

    mod_p = _ada(c_prompt, w_ada_b, row(b_ada[0]))
    mod_s = _ada(c_sample, w_ada_b, row(b_ada[0]))
    mod_p3 = mod_p.reshape(b, 1, mod_p.shape[-1])

    cos_p, sin_p = _rope_tables(jnp.arange(s))
    q, kb, vt, k_p, v_p, u_p = _p_inproj(x_prompt, mod_p3, g_pre, w_in_b, cos_p, sin_p, tm)
    o_p = _p_attn(lam_params, g_subln[0].reshape(DV, 1), q, kb, vt, lam_init, tm)
    y_p = _p_post(x_prompt, o_p, u_p, mod_p3, w_dw[0], b_dw2, g_cln2, b_cln2, g_pm, g_pf, g_qf,
                  w_out_b, w_up_b, w_down_b, tm)

    cos_s, sin_s = _rope_tables(past + jnp.arange(t))
    xs = x_sample.reshape(db, d)
    q_s, k_s, v_s, u_s = _s_inproj(xs, mod_s, g_pre, w_in_b, cos_s, sin_s)
    ck = cache_k[0].reshape(n_phys, page * N_HEADS, HEAD_W)
    cv = cache_v[0].reshape(n_phys, page * N_HEADS, DV)
    o_s = _s_attn(page_table, lam_params, q_s, k_s, v_s, ck, cv, lam_init)
    state = state_conv[0]
    y_s = _s_post(xs, o_s, u_s, jnp.transpose(state, (1, 0, 2)), mod_s, row(g_subln[0]),
                  w_dw[0], b_dw2, g_cln2, b_cln2, g_pm, g_pf, g_qf, w_out_b, w_up_b, w_down_b, lam_init)

    k_prompt = k_p.reshape(1, b, s, N_HEADS, 2 * DK)
    v_prompt = v_p.reshape(1, b, s, N_HEADS, DV)
    conv_prompt = u_p[:, s - (CONV_K - 1):, :][None]
    k_sample = k_s.reshape(1, db, 1, N_HEADS, 2 * DK)
    v_sample = v_s.reshape(1, db, 1, N_HEADS, DV)
    conv_sample = jnp.concatenate([state[:, 1:, :], u_s[:, None, :]], axis=1)[None]
    return (y_p, y_s.reshape(db, 1, d), k_prompt, v_prompt, conv_prompt, k_sample, v_sample, conv_sample)
```

```python
import functools
import math

import jax
import jax.numpy as jnp
from jax import lax
from jax.experimental import pallas as pl
from jax.experimental.pallas import tpu as pltpu

F32 = jnp.float32
BF16 = jnp.bfloat16

N_HEADS = 4
DK = 64
DV = 2 * DK
HEAD_W = 2 * DK
QK_W = N_HEADS * 2 * DK
V_W = N_HEADS * DV
CONV_CH = 512
CONV_K = 31
ROPE_THETA = 10000.0
ATTN_SCALE = DK ** -0.5
LOG2_E = math.log2(math.e)
NORM_EPS = 1e-6
NEG_INF = -1e30

LANES = 128
SUBLANES = 8
ROW_TILE = 512
FF_CHUNK = 1024
ATTN_UNROLL = 4
SCORE_PAD = LANES
ADA_COL_TILE = 1024
PAGES_PER_STEP = 8
SEQS_PER_STEP = 2
CONV_HALO = 32
VMEM_LIMIT = 56 * 1024 * 1024


def _rms(x, g):
    return x * lax.rsqrt(jnp.mean(x * x, axis=-1, keepdims=True) + NORM_EPS) * g


def _sigmoid(x):
    return 1.0 / (1.0 + jnp.exp(-x))


def _rope_128(x, cos, sin_signed):
    lane = lax.broadcasted_iota(jnp.int32, x.shape, 1)
    first_half = (lane & (DK - 1)) < (DK // 2)
    partner = jnp.where(first_half, pltpu.roll(x, LANES - DK // 2, 1), pltpu.roll(x, DK // 2, 1))
    return x * cos + partner * sin_signed


def _mlp(h2b, w_up_ref, w_down_ref, d_ff):
    f = None
    for c in range(d_ff // FF_CHUNK):
        hid = jnp.dot(h2b, w_up_ref[:, c * FF_CHUNK:(c + 1) * FF_CHUNK], preferred_element_type=F32)
        hid = jnp.maximum(hid, 0.0)
        hid = (hid * hid).astype(BF16)
        part = jnp.dot(hid, w_down_ref[c * FF_CHUNK:(c + 1) * FF_CHUNK, :], preferred_element_type=F32)
        f = part if f is None else f + part
    return f


def _causal_conv(uext, w_ref, bias, tm):
    span = tm + CONV_HALO
    base = CONV_HALO - (CONV_K - 1)
    conv = jnp.zeros((tm, CONV_CH), F32) + bias
    for r in range(SUBLANES):
        taps = [j for j in range(CONV_K) if (base + j) % SUBLANES == r]
        if not taps:
            continue
        shifted = uext if r == 0 else pltpu.roll(uext, span - r, 0)
        for j in taps:
            lo = base + j - r
            conv = conv + w_ref[j:j + 1, :] * shifted[lo:lo + tm, :]
    return conv


def _conv_ln_swish(conv, g_cln, b_cln):
    mu = jnp.mean(conv, axis=-1, keepdims=True)
    cen = conv - mu
    var = jnp.mean(cen * cen, axis=-1, keepdims=True)
    y = cen * lax.rsqrt(var + NORM_EPS) * g_cln + b_cln
    return y * _sigmoid(y)


def _lambda(lq1, lk1, lq2, lk2, lam_init):
    s1 = jnp.sum(lq1 * lk1, axis=-1, keepdims=True)
    s2 = jnp.sum(lq2 * lk2, axis=-1, keepdims=True)
    return jnp.exp(s1) - jnp.exp(s2) + lam_init


def _ada_kernel(c_ref, w_ref, b_ref, o_ref):
    c = c_ref[...]
    s = (c * _sigmoid(c)).astype(BF16)
    o_ref[...] = jnp.dot(s, w_ref[...], preferred_element_type=F32) + b_ref[...]


def _ada(c, w_ada_b, b_ada):
    rows, d = c.shape
    n = w_ada_b.shape[1]
    return pl.pallas_call(
        _ada_kernel,
        grid=(n // ADA_COL_TILE,),
        in_specs=[
            pl.BlockSpec((rows, d), lambda j: (0, 0)),
            pl.BlockSpec((d, ADA_COL_TILE), lambda j: (0, j)),
            pl.BlockSpec((1, ADA_COL_TILE), lambda j: (0, j)),
        ],
        out_specs=pl.BlockSpec((rows, ADA_COL_TILE), lambda j: (0, j)),
        out_shape=jax.ShapeDtypeStruct((rows, n), F32),
        name="ada",
    )(c, w_ada_b, b_ada)


def _inproj_math(x, shift, scale, g_pre, w_in_ref, cos, sin_signed, q_scale):
    h = _rms(x, g_pre) * (1.0 + scale) + shift
    z = jnp.dot(h.astype(BF16), w_in_ref[...], preferred_element_type=F32)
    qs, ks, vs = [], [], []
    for hh in range(N_HEADS):
        lo = hh * HEAD_W
        qs.append(_rope_128(z[:, lo:lo + HEAD_W], cos, sin_signed) * q_scale)
        ks.append(_rope_128(z[:, QK_W + lo:QK_W + lo + HEAD_W], cos, sin_signed))
        vs.append(z[:, 2 * QK_W + lo:2 * QK_W + lo + DV])
    ua = z[:, 2 * QK_W + V_W:2 * QK_W + V_W + CONV_CH]
    ug = z[:, 2 * QK_W + V_W + CONV_CH:]
    u = ua * _sigmoid(ug)
    return qs, ks, vs, u


def _p_inproj_kernel(x_ref, mod_ref, g_ref, w_ref, cos_ref, sin_ref,
                     q_ref, kb_ref, vt_ref, k_ref, v_ref, u_ref):
    d_model = x_ref.shape[-1]
    tm = x_ref.shape[1]
    shift = mod_ref[0, :, 0:d_model]
    scale = mod_ref[0, :, d_model:2 * d_model]
    qs, ks, vs, u = _inproj_math(x_ref[0], shift, scale, g_ref[...], w_ref, cos_ref[...], sin_ref[...],
                                 ATTN_SCALE * LOG2_E)
    for hh in range(N_HEADS):
        q_ref[0, hh] = qs[hh].astype(BF16)
        kb_ref[0, hh] = ks[hh].astype(BF16)
        k_ref[0, pl.ds(hh, tm, stride=N_HEADS), :] = ks[hh]
        v_ref[0, pl.ds(hh, tm, stride=N_HEADS), :] = vs[hh]
        vt_ref[0, hh, 0] = vs[hh].T.astype(BF16)
    u_ref[0] = u


def _p_inproj(x, mod3, g_pre, w_in_b, cos, sin_signed, tm):
    b, s, d = x.shape
    nt = s // tm
    d_in = w_in_b.shape[1]
    return pl.pallas_call(
        _p_inproj_kernel,
        grid=(b, nt),
        in_specs=[
            pl.BlockSpec((1, tm, d), lambda bi, i: (bi, i, 0)),
            pl.BlockSpec((1, 1, mod3.shape[-1]), lambda bi, i: (bi, 0, 0)),
            pl.BlockSpec((1, d), lambda bi, i: (0, 0)),
            pl.BlockSpec((d, d_in), lambda bi, i: (0, 0)),
            pl.BlockSpec((tm, LANES), lambda bi, i: (i, 0)),
            pl.BlockSpec((tm, LANES), lambda bi, i: (i, 0)),
        ],
        out_specs=[
            pl.BlockSpec((1, N_HEADS, tm, HEAD_W), lambda bi, i: (bi, 0, i, 0)),
            pl.BlockSpec((1, N_HEADS, tm, HEAD_W), lambda bi, i: (bi, 0, i, 0)),
            pl.BlockSpec((1, N_HEADS, 1, DV, tm), lambda bi, i: (bi, 0, i, 0, 0)),
            pl.BlockSpec((1, tm * N_HEADS, HEAD_W), lambda bi, i: (bi, i, 0)),
            pl.BlockSpec((1, tm * N_HEADS, DV), lambda bi, i: (bi, i, 0)),
            pl.BlockSpec((1, tm, CONV_CH), lambda bi, i: (bi, i, 0)),
        ],
        out_shape=[
            jax.ShapeDtypeStruct((b, N_HEADS, s, HEAD_W), BF16),
            jax.ShapeDtypeStruct((b, N_HEADS, s, HEAD_W), BF16),
            jax.ShapeDtypeStruct((b, N_HEADS, nt, DV, tm), BF16),
            jax.ShapeDtypeStruct((b, s * N_HEADS, HEAD_W), F32),
            jax.ShapeDtypeStruct((b, s * N_HEADS, DV), F32),
            jax.ShapeDtypeStruct((b, s, CONV_CH), F32),
        ],
        compiler_params=pltpu.CompilerParams(
            dimension_semantics=("parallel", "arbitrary"), vmem_limit_bytes=VMEM_LIMIT),
        name="p_inproj",
    )(x, mod3, g_pre, w_in_b, cos, sin_signed)


def _p_attn_kernel(lq1_ref, lk1_ref, lq2_ref, lk2_ref, g_ref, q_ref, k_ref, vt_ref, o_ref,
                   m_ref, l_ref, acc0_ref, acc1_ref, sa_ref, sb_ref, *, lam_init, tq):
    qi = pl.program_id(2)
    qt = q_ref[0, 0].astype(F32).T
    row = lax.broadcasted_iota(jnp.int32, qt.shape, 0)
    zero = jnp.zeros_like(qt)
    q2t = jnp.concatenate([jnp.where(row < DK, qt, zero), jnp.where(row >= DK, qt, zero)],
                          axis=1).astype(BF16)

    m_ref[...] = jnp.full(m_ref.shape, NEG_INF, F32)
    l_ref[...] = jnp.zeros(l_ref.shape, F32)
    acc0_ref[...] = jnp.zeros(acc0_ref.shape, F32)
    acc1_ref[...] = jnp.zeros(acc1_ref.shape, F32)

    def scores(c, dst_ref):
        start = pl.multiple_of(c * tq, tq)
        kc = k_ref[0, 0, pl.ds(start, tq), :]
        dst_ref[:, 0:2 * tq] = jnp.dot(kc, q2t, preferred_element_type=F32)

    def softmax_pv(src_ref, c, masked):
        st = src_ref[:, 0:2 * tq]
        vtc = vt_ref[0, 0, c]
        if masked:
            row = lax.broadcasted_iota(jnp.int32, st.shape, 0)
            col = lax.broadcasted_iota(jnp.int32, st.shape, 1)
            col = jnp.where(col >= tq, col - tq, col)
            st = jnp.where(row <= col, st, NEG_INF)
        m_old = m_ref[...]
        m_new = jnp.maximum(m_old, jnp.max(st, axis=0, keepdims=True))
        alpha = jnp.exp2(m_old - m_new)
        p = jnp.exp2(st - m_new)
        l_ref[...] = alpha * l_ref[...] + jnp.sum(p, axis=0, keepdims=True)
        m_ref[...] = m_new
        pb = p.astype(BF16)
        acc0_ref[...] = acc0_ref[...] * alpha[:, :tq] + jnp.dot(vtc, pb[:, :tq], preferred_element_type=F32)
        acc1_ref[...] = acc1_ref[...] * alpha[:, tq:] + jnp.dot(vtc, pb[:, tq:], preferred_element_type=F32)

    bufs = (sa_ref, sb_ref)

    def run(c0, n, masked_last, prefetch_after):
        for t in range(n):
            if t + 1 < n or prefetch_after:
                scores(c0 + t + 1, bufs[(t + 1) % 2])
            softmax_pv(bufs[t % 2], c0 + t, masked=masked_last and t == n - 1)

    scores(0, sa_ref)

    def body(j, carry):
        run(ATTN_UNROLL * j, ATTN_UNROLL, masked_last=False, prefetch_after=True)
        return carry

    lax.fori_loop(0, qi // ATTN_UNROLL, body, 0)

    for r in range(ATTN_UNROLL):
        @pl.when(qi % ATTN_UNROLL == r)
        def _():
            run(qi - r, r + 1, masked_last=True, prefetch_after=False)

    lam = _lambda(lq1_ref[...], lk1_ref[...], lq2_ref[...], lk2_ref[...], lam_init)
    l = l_ref[...]
    ot = acc0_ref[...] * (1.0 / l[:, :tq]) - lam * (acc1_ref[...] * (1.0 / l[:, tq:]))
    ms = jnp.mean(ot * ot, axis=0, keepdims=True)
    ot = ot * lax.rsqrt(ms + NORM_EPS) * g_ref[...] * (1.0 - lam_init)
    o_ref[0] = ot.T.astype(BF16)


def _p_attn(lam_params, g_col, q, kb, vt, lam_init, tq):
    b, _, s, _ = q.shape
    nq = s // tq
    lam_spec = pl.BlockSpec((1, DK), lambda bi, hi, i: (0, 0))
    return pl.pallas_call(
        functools.partial(_p_attn_kernel, lam_init=lam_init, tq=tq),
        grid=(b, N_HEADS, nq),
        in_specs=[
            lam_spec, lam_spec, lam_spec, lam_spec,
            pl.BlockSpec((DV, 1), lambda bi, hi, i: (0, 0)),
            pl.BlockSpec((1, 1, tq, HEAD_W), lambda bi, hi, i: (bi, hi, i, 0)),
            pl.BlockSpec((1, 1, s, HEAD_W), lambda bi, hi, i: (bi, hi, 0, 0)),
            pl.BlockSpec((1, 1, nq, DV, tq), lambda bi, hi, i: (bi, hi, 0, 0, 0)),
        ],
        out_specs=pl.BlockSpec((1, tq, DV), lambda bi, hi, i: (bi, i, hi)),
        out_shape=jax.ShapeDtypeStruct((b, s, V_W), BF16),
        scratch_shapes=[
            pltpu.VMEM((1, 2 * tq), F32),
            pltpu.VMEM((1, 2 * tq), F32),
            pltpu.VMEM((DV, tq), F32),
            pltpu.VMEM((DV, tq), F32),
            pltpu.VMEM((tq, 2 * tq + SCORE_PAD), F32),
            pltpu.VMEM((tq, 2 * tq + SCORE_PAD), F32),
        ],
        compiler_params=pltpu.CompilerParams(
            dimension_semantics=("parallel", "parallel", "arbitrary"), vmem_limit_bytes=VMEM_LIMIT),
        name="p_attn",
    )(*lam_params, g_col, q, kb, vt)


def _post_math(x, mixb, mod, gains, w_out_ref, w_up_ref, w_down_ref):
    d = x.shape[-1]
    g_post_mix, g_pre_ffn, g_post_ffn = gains
    gt1 = mod[:, 2 * d:3 * d]
    sh2 = mod[:, 3 * d:4 * d]
    sc2 = mod[:, 4 * d:5 * d]
    gt2 = mod[:, 5 * d:6 * d]
    m = jnp.dot(mixb, w_out_ref[...], preferred_element_type=F32)
    x1 = x + gt1 * _rms(m, g_post_mix)
    h2 = _rms(x1, g_pre_ffn) * (1.0 + sc2) + sh2
    f = _mlp(h2.astype(BF16), w_up_ref, w_down_ref, w_up_ref.shape[1])
    return x1 + gt2 * _rms(f, g_post_ffn)


def _p_post_kernel(x_ref, o_ref, u_ref, uprev_ref, mod_ref, wdw_ref, bdw_ref, gcln_ref, bcln_ref,
                   gpm_ref, gpf_ref, gqf_ref, w_out_ref, w_up_ref, w_down_ref, y_ref, uext_ref, *, tm):
    i = pl.program_id(1)

    @pl.when(i == 0)
    def _():
        uext_ref[0:CONV_HALO, :] = jnp.zeros((CONV_HALO, CONV_CH), F32)

    @pl.when(i > 0)
    def _():
        uext_ref[0:CONV_HALO, :] = uprev_ref[0]

    uext_ref[CONV_HALO:, :] = u_ref[0]
    conv = _causal_conv(uext_ref[...], wdw_ref, bdw_ref[...], tm)
    cv = _conv_ln_swish(conv, gcln_ref[...], bcln_ref[...])
    mixb = jnp.concatenate([o_ref[0], cv.astype(BF16)], axis=-1)
    y_ref[0] = _post_math(x_ref[0], mixb, mod_ref[0], (gpm_ref[...], gpf_ref[...], gqf_ref[...]),
                          w_out_ref, w_up_ref, w_down_ref)


def _const_spec(shape, nargs):
    zeros = (0,) * len(shape)
    if nargs == 2:
        return pl.BlockSpec(shape, lambda a, b: zeros, pipeline_mode=pl.Buffered(1))
    return pl.BlockSpec(shape, lambda a: zeros, pipeline_mode=pl.Buffered(1))


def _p_post(x, o, u, mod3, w_dw, b_dw, g_cln, b_cln, g_post_mix, g_pre_ffn, g_post_ffn,
            w_out_b, w_up_b, w_down_b, tm):
    b, s, d = x.shape
    nt = s // tm
    d_ff = w_up_b.shape[1]
    halo_blocks = tm // CONV_HALO
    vec = lambda n: pl.BlockSpec((1, n), lambda bi, i: (0, 0))
    return pl.pallas_call(
        functools.partial(_p_post_kernel, tm=tm),
        grid=(b, nt),
        in_specs=[
            pl.BlockSpec((1, tm, d), lambda bi, i: (bi, i, 0)),
            pl.BlockSpec((1, tm, V_W), lambda bi, i: (bi, i, 0)),
            pl.BlockSpec((1, tm, CONV_CH), lambda bi, i: (bi, i, 0)),
            pl.BlockSpec((1, CONV_HALO, CONV_CH), lambda bi, i: (bi, jnp.maximum(i * halo_blocks - 1, 0), 0)),
            pl.BlockSpec((1, 1, mod3.shape[-1]), lambda bi, i: (bi, 0, 0)),
            pl.BlockSpec((CONV_K, CONV_CH), lambda bi, i: (0, 0)),
            vec(CONV_CH), vec(CONV_CH), vec(CONV_CH),
            vec(d), vec(d), vec(d),
            _const_spec((V_W + CONV_CH, d), 2),
            _const_spec((d, d_ff), 2),
            _const_spec((d_ff, d), 2),
        ],
        out_specs=pl.BlockSpec((1, tm, d), lambda bi, i: (bi, i, 0)),
        out_shape=jax.ShapeDtypeStruct((b, s, d), F32),
        scratch_shapes=[pltpu.VMEM((tm + CONV_HALO, CONV_CH), F32)],
        compiler_params=pltpu.CompilerParams(
            dimension_semantics=("parallel", "arbitrary"), vmem_limit_bytes=VMEM_LIMIT),
        name="p_post",
    )(x, o, u, u, mod3, w_dw, b_dw, g_cln, b_cln, g_post_mix, g_pre_ffn, g_post_ffn,
      w_out_b, w_up_b, w_down_b)


def _s_inproj_kernel(x_ref, mod_ref, g_ref, w_ref, cos_ref, sin_ref, q_ref, k_ref, v_ref, u_ref):
    d_model = x_ref.shape[-1]
    shift = mod_ref[:, 0:d_model]
    scale = mod_ref[:, d_model:2 * d_model]
    qs, ks, vs, u = _inproj_math(x_ref[...], shift, scale, g_ref[...], w_ref, cos_ref[...], sin_ref[...],
                                 ATTN_SCALE)
    for hh in range(N_HEADS):
        lo = hh * HEAD_W
        q_ref[:, lo:lo + HEAD_W] = qs[hh]
        k_ref[:, lo:lo + HEAD_W] = ks[hh]
        v_ref[:, lo:lo + DV] = vs[hh]
    u_ref[...] = u


def _s_inproj(x, mod, g_pre, w_in_b, cos_row, sin_row):
    rows, d = x.shape
    full = lambda a: pl.BlockSpec(a.shape, lambda i: (0,) * a.ndim)
    out = jax.ShapeDtypeStruct((rows, QK_W), F32)
    return pl.pallas_call(
        _s_inproj_kernel,
        grid=(1,),
        in_specs=[full(x), full(mod), full(g_pre), full(w_in_b), full(cos_row), full(sin_row)],
        out_specs=[pl.BlockSpec((rows, QK_W), lambda i: (0, 0))] * 4,
        out_shape=[out, out, out, out],
        compiler_params=pltpu.CompilerParams(vmem_limit_bytes=VMEM_LIMIT),
        name="s_inproj",
    )(x, mod, g_pre, w_in_b, cos_row, sin_row)


def _s_attn_kernel(pt_ref, lq1_ref, lk1_ref, lq2_ref, lk2_ref, q_ref, kn_ref, vn_ref, *rest,
                   lam_init, pps, spg):
    k_refs = rest[:spg * pps]
    v_refs = rest[spg * pps:2 * spg * pps]
    o_ref = rest[2 * spg * pps]
    m_ref, l_ref, acc_ref = rest[2 * spg * pps + 1:]
    step = pl.program_id(1)
    groups = 2 * N_HEADS
    n_rows = pps * k_refs[0].shape[1]
    grp = lax.broadcasted_iota(jnp.int32, (groups, HEAD_W), 0)
    lane = lax.broadcasted_iota(jnp.int32, (groups, HEAD_W), 1)
    own_component = (lane >= DK) == ((grp & 1) == 1)
    row_grp = lax.broadcasted_iota(jnp.int32, (groups, n_rows), 0)
    col = lax.broadcasted_iota(jnp.int32, (groups, n_rows), 1)
    own_head = (col & (N_HEADS - 1)) == (row_grp >> 1)

    def per_group(row):
        return jnp.concatenate([row[:, (g // 2) * HEAD_W:(g // 2 + 1) * HEAD_W] for g in range(groups)], axis=0)

    for r in range(spg):
        qg = jnp.where(own_component, per_group(q_ref[0, r:r + 1, :]), 0.0)

        @pl.when(step == 0)
        def _():
            m_ref[r] = jnp.sum(qg * per_group(kn_ref[0, r:r + 1, :]), axis=-1, keepdims=True)
            l_ref[r] = jnp.ones(l_ref.shape[1:], F32)
            acc_ref[r] = per_group(vn_ref[0, r:r + 1, :])

        kd = jnp.concatenate([kr[0] for kr in k_refs[r * pps:(r + 1) * pps]], axis=0).astype(BF16)
        vd = jnp.concatenate([vr[0] for vr in v_refs[r * pps:(r + 1) * pps]], axis=0).astype(BF16)
        s = lax.dot_general(qg.astype(BF16), kd, (((1,), (1,)), ((), ())), preferred_element_type=F32)
        s = jnp.where(own_head, s, NEG_INF)
        m_old = m_ref[r]
        m_new = jnp.maximum(m_old, jnp.max(s, axis=-1, keepdims=True))
        alpha = jnp.exp(m_old - m_new)
        p = jnp.exp(s - m_new)
        l_ref[r] = alpha * l_ref[r] + jnp.sum(p, axis=-1, keepdims=True)
        m_ref[r] = m_new
        acc_ref[r] = acc_ref[r] * alpha + jnp.dot(p.astype(BF16), vd, preferred_element_type=F32)

    @pl.when(step == pl.num_programs(1) - 1)
    def _():
        lam = _lambda(lq1_ref[...], lk1_ref[...], lq2_ref[...], lk2_ref[...], lam_init)
        for r in range(spg):
            on = acc_ref[r] * (1.0 / l_ref[r])
            for hh in range(N_HEADS):
                o_ref[0, r:r + 1, hh * DV:(hh + 1) * DV] = on[2 * hh:2 * hh + 1, :] - lam * on[2 * hh + 1:2 * hh + 2, :]


def _s_attn(page_table, lam_params, q, k_new, v_new, cache_k, cache_v, lam_init):
    db, n_pages = page_table.shape
    pps = math.gcd(PAGES_PER_STEP, n_pages)
    spg = math.gcd(SEQS_PER_STEP, db)
    page_rows = cache_k.shape[1]
    q3, kn3, vn3 = (a.reshape(db // spg, spg, a.shape[-1]) for a in (q, k_new, v_new))
    lam_spec = pl.BlockSpec((1, DK), lambda b, s, pt: (0, 0))
    row_spec = pl.BlockSpec((1, spg, QK_W), lambda b, s, pt: (b, 0, 0))

    def page_spec(r, i):
        return pl.BlockSpec((1, page_rows, HEAD_W), lambda b, s, pt: (pt[b * spg + r, s * pps + i], 0, 0))

    page_specs = [page_spec(r, i) for r in range(spg) for i in range(pps)]
    grid_spec = pltpu.PrefetchScalarGridSpec(
        num_scalar_prefetch=1,
        grid=(db // spg, n_pages // pps),
        in_specs=[lam_spec] * 4 + [row_spec] * 3 + page_specs * 2,
        out_specs=pl.BlockSpec((1, spg, V_W), lambda b, s, pt: (b, 0, 0)),
        scratch_shapes=[
            pltpu.VMEM((spg, 2 * N_HEADS, 1), F32),
            pltpu.VMEM((spg, 2 * N_HEADS, 1), F32),
            pltpu.VMEM((spg, 2 * N_HEADS, DV), F32),
        ],
    )
    n_page_args = spg * pps
    out = pl.pallas_call(
        functools.partial(_s_attn_kernel, lam_init=lam_init, pps=pps, spg=spg),
        grid_spec=grid_spec,
        out_shape=jax.ShapeDtypeStruct((db // spg, spg, V_W), F32),
        compiler_params=pltpu.CompilerParams(
            dimension_semantics=("parallel", "arbitrary"), vmem_limit_bytes=VMEM_LIMIT),
        name="s_attn",
    )(page_table, *lam_params, q3, kn3, vn3, *([cache_k] * n_page_args), *([cache_v] * n_page_args))
    return out.reshape(db, V_W)


def _s_post_kernel(x_ref, o_ref, u_ref, st_ref, mod_ref, gsub_ref, wdw_ref, bdw_ref, gcln_ref, bcln_ref,
                   gpm_ref, gpf_ref, gqf_ref, w_out_ref, w_up_ref, w_down_ref, y_ref, *, lam_init):
    o = o_ref[...]
    heads = []
    for hh in range(N_HEADS):
        oh = o[:, hh * DV:(hh + 1) * DV]
        heads.append(_rms(oh, gsub_ref[...]) * (1.0 - lam_init))
    conv = bdw_ref[...] + wdw_ref[CONV_K - 1:CONV_K, :] * u_ref[...]
    for j in range(CONV_K - 1):
        conv = conv + wdw_ref[j:j + 1, :] * st_ref[j]
    cv = _conv_ln_swish(conv, gcln_ref[...], bcln_ref[...])
    mixb = jnp.concatenate(heads + [cv], axis=-1).astype(BF16)
    y_ref[...] = _post_math(x_ref[...], mixb, mod_ref[...], (gpm_ref[...], gpf_ref[...], gqf_ref[...]),
                            w_out_ref, w_up_ref, w_down_ref)


def _s_post(x, o, u, state_t, mod, g_subln, w_dw, b_dw, g_cln, b_cln, g_post_mix, g_pre_ffn, g_post_ffn,
            w_out_b, w_up_b, w_down_b, lam_init):
    rows, d = x.shape
    args = (x, o, u, state_t, mod, g_subln, w_dw, b_dw, g_cln, b_cln, g_post_mix, g_pre_ffn, g_post_ffn,
            w_out_b, w_up_b, w_down_b)
    return pl.pallas_call(
        functools.partial(_s_post_kernel, lam_init=lam_init),
        grid=(1,),
        in_specs=[_const_spec(a.shape, 1) for a in args],
        out_specs=pl.BlockSpec((rows, d), lambda i: (0, 0)),
        out_shape=jax.ShapeDtypeStruct((rows, d), F32),
        compiler_params=pltpu.CompilerParams(vmem_limit_bytes=VMEM_LIMIT),
        name="s_post",
    )(*args)


def _rope_tables(pos):
    half = DK // 2
    inv_freq = ROPE_THETA ** (-jnp.arange(half, dtype=F32) / half)
    ang = pos.astype(F32)[:, None] * inv_freq[None, :]
    cos, sin = jnp.cos(ang), jnp.sin(ang)
    reps = LANES // DK
    cos128 = jnp.tile(jnp.concatenate([cos, cos], axis=-1), (1, reps))
    sin128 = jnp.tile(jnp.concatenate([-sin, sin], axis=-1), (1, reps))
    return cos128, sin128


def kernel(x_prompt, x_sample, c_prompt, c_sample, cache_k, cache_v, state_conv, page_table, w_in, w_out, lam_q1, lam_k1, lam_q2, lam_k2, g_subln, w_dw, b_dw, g_cln, b_cln, g_pre_mix, g_post_mix, g_pre_ffn, g_post_ffn, w_ada, b_ada, w_up, w_down):
    b, s, d = x_prompt.shape
    db, t, _ = x_sample.shape
    depth = w_in.shape[0]
    assert depth == 1 and t == 1, "kernel supports one layer and one new token per sample sequence"
    n_phys, page = cache_k.shape[1], cache_k.shape[2]
    past = page_table.shape[1] * page
    tm = min(ROW_TILE, s)
    assert s % tm == 0 and tm % CONV_HALO == 0
    lam_init = 0.8 - 0.6 * math.exp(-0.3 * 0)

    w_in_b = w_in[0].astype(BF16)
    w_out_b = w_out[0].astype(BF16)
    w_up_b = w_up[0].astype(BF16)
    w_down_b = w_down[0].astype(BF16)
    w_ada_b = w_ada[0].astype(BF16)
    row = lambda a: a.reshape(1, -1)
    lam_params = (row(lam_q1[0]), row(lam_k1[0]), row(lam_q2[0]), row(lam_k2[0]))
    g_pre, g_pm, g_pf, g_qf = row(g_pre_mix[0]), row(g_post_mix[0]), row(g_pre_ffn[0]), row(g_post_ffn[0])
    b_dw2, g_cln2, b_cln2 = row(b_dw[0]), row(g_cln[0]), row(b_cln[0])

    mod_p = _ada(c_prompt, w_ada_b, row(b_ada[0]))
    mod_s = _ada(c_sample, w_ada_b, row(b_ada[0]))
    mod_p3 = mod_p.reshape(b, 1, mod_p.shape[-1])

    cos_p, sin_p = _rope_tables(jnp.arange(s))
    q, kb, vt, k_p, v_p, u_p = _p_inproj(x_prompt, mod_p3, g_pre, w_in_b, cos_p, sin_p, tm)
    o_p = _p_attn(lam_params, g_subln[0].reshape(DV, 1), q, kb, vt, lam_init, tm)
    y_p = _p_post(x_prompt, o_p, u_p, mod_p3, w_dw[0], b_dw2, g_cln2, b_cln2, g_pm, g_pf, g_qf,
                  w_out_b, w_up_b, w_down_b, tm)

    cos_s, sin_s = _rope_tables(past + jnp.arange(t))
    xs = x_sample.reshape(db, d)
    q_s, k_s, v_s, u_s = _s_inproj(xs, mod_s, g_pre, w_in_b, cos_s, sin_s)
    ck = cache_k[0].reshape(n_phys, page * N_HEADS, HEAD_W)
    cv = cache_v[0].reshape(n_phys, page * N_HEADS, DV)
    o_s = _s_attn(page_table, lam_params, q_s, k_s, v_s, ck, cv, lam_init)
    state = state_conv[0]
    y_s = _s_post(xs, o_s, u_s, jnp.transpose(state, (1, 0, 2)), mod_s, row(g_subln[0]),
                  w_dw[0], b_dw2, g_cln2, b_cln2, g_pm, g_pf, g_qf, w_out_b, w_up_b, w_down_b, lam_init)

    k_prompt = k_p.reshape(1, b, s, N_HEADS, 2 * DK)
    v_prompt = v_p.reshape(1, b, s, N_HEADS, DV)
    conv_prompt = u_p[:, s - (CONV_K - 1):, :][None]
    k_sample = k_s.reshape(1, db, 1, N_HEADS, 2 * DK)
    v_sample = v_s.reshape(1, db, 1, N_HEADS, DV)
    conv_sample = jnp.concatenate([state[:, 1:, :], u_s[:, None, :]], axis=1)[None]
    return (y_p, y_s.reshape(db, 1, d), k_prompt, v_prompt, conv_prompt, k_sample, v_sample, conv_sample)
```

```python
import functools
import math

import jax
import jax.numpy as jnp
from jax import lax
from jax.experimental import pallas as pl
from jax.experimental.pallas import tpu as pltpu

F32 = jnp.float32
BF16 = jnp.bfloat16

N_HEADS = 4
DK = 64
DV = 2 * DK
HEAD_W = 2 * DK
QK_W = N_HEADS * 2 * DK
V_W = N_HEADS * DV
CONV_CH = 512
CONV_K = 31
ROPE_THETA = 10000.0
ATTN_SCALE = DK ** -0.5
LOG2_E = math.log2(math.e)
NORM_EPS = 1e-6
NEG_INF = -1e30

LANES = 128
SUBLANES = 8
ROW_TILE = 512
FF_CHUNK = 1024
ATTN_UNROLL = 4
SCORE_PAD = LANES
ADA_COL_TILE = 1024
PAGES_PER_STEP = 8
SEQS_PER_STEP = 2
CONV_HALO = 32
VMEM_LIMIT = 56 * 1024 * 1024


def _rms(x, g):
    return x * lax.rsqrt(jnp.mean(x * x, axis=-1, keepdims=True) + NORM_EPS) * g


def _sigmoid(x):
    return 1.0 / (1.0 + jnp.exp(-x))


def _rope_128(x, cos, sin_signed):
    lane = lax.broadcasted_iota(jnp.int32, x.shape, 1)
    first_half = (lane & (DK - 1)) < (DK // 2)
    partner = jnp.where(first_half, pltpu.roll(x, LANES - DK // 2, 1), pltpu.roll(x, DK // 2, 1))
    return x * cos + partner * sin_signed


def _mlp(h2b, w_up_ref, w_down_ref, d_ff):
    f = None
    for c in range(d_ff // FF_CHUNK):
        hid = jnp.dot(h2b, w_up_ref[:, c * FF_CHUNK:(c + 1) * FF_CHUNK], preferred_element_type=F32)
        hid = jnp.maximum(hid, 0.0)
        hid = (hid * hid).astype(BF16)
        part = jnp.dot(hid, w_down_ref[c * FF_CHUNK:(c + 1) * FF_CHUNK, :], preferred_element_type=F32)
        f = part if f is None else f + part
    return f


def _causal_conv(uext, w_ref, bias, tm):
    span = tm + CONV_HALO
    base = CONV_HALO - (CONV_K - 1)
    conv = jnp.zeros((tm, CONV_CH), F32) + bias
    for r in range(SUBLANES):
        taps = [j for j in range(CONV_K) if (base + j) % SUBLANES == r]
        if not taps:
            continue
        shifted = uext if r == 0 else pltpu.roll(uext, span - r, 0)
        for j in taps:
            lo = base + j - r
            conv = conv + w_ref[j:j + 1, :] * shifted[lo:lo + tm, :]
    return conv


def _conv_ln_swish(conv, g_cln, b_cln):
    mu = jnp.mean(conv, axis=-1, keepdims=True)
    cen = conv - mu
    var = jnp.mean(cen * cen, axis=-1, keepdims=True)
    y = cen * lax.rsqrt(var + NORM_EPS) * g_cln + b_cln
    return y * _sigmoid(y)


def _lambda(lq1, lk1, lq2, lk2, lam_init):
    s1 = jnp.sum(lq1 * lk1, axis=-1, keepdims=True)
    s2 = jnp.sum(lq2 * lk2, axis=-1, keepdims=True)
    return jnp.exp(s1) - jnp.exp(s2) + lam_init


def _ada_kernel(c_ref, w_ref, b_ref, o_ref):
    c = c_ref[...]
    s = (c * _sigmoid(c)).astype(BF16)
    o_ref[...] = jnp.dot(s, w_ref[...], preferred_element_type=F32) + b_ref[...]


def _ada(c, w_ada_b, b_ada):
    rows, d = c.shape
    n = w_ada_b.shape[1]
    return pl.pallas_call(
        _ada_kernel,
        grid=(n // ADA_COL_TILE,),
        in_specs=[
            pl.BlockSpec((rows, d), lambda j: (0, 0)),
            pl.BlockSpec((d, ADA_COL_TILE), lambda j: (0, j)),
            pl.BlockSpec((1, ADA_COL_TILE), lambda j: (0, j)),
        ],
        out_specs=pl.BlockSpec((rows, ADA_COL_TILE), lambda j: (0, j)),
        out_shape=jax.ShapeDtypeStruct((rows, n), F32),
        name="ada",
    )(c, w_ada_b, b_ada)


def _inproj_math(x, shift, scale, g_pre, w_in_ref, cos, sin_signed, q_scale):
    h = _rms(x, g_pre) * (1.0 + scale) + shift
    z = jnp.dot(h.astype(BF16), w_in_ref[...], preferred_element_type=F32)
    qs, ks, vs = [], [], []
    for hh in range(N_HEADS):
        lo = hh * HEAD_W
        qs.append(_rope_128(z[:, lo:lo + HEAD_W], cos, sin_signed) * q_scale)
        ks.append(_rope_128(z[:, QK_W + lo:QK_W + lo + HEAD_W], cos, sin_signed))
        vs.append(z[:, 2 * QK_W + lo:2 * QK_W + lo + DV])
    ua = z[:, 2 * QK_W + V_W:2 * QK_W + V_W + CONV_CH]
    ug = z[:, 2 * QK_W + V_W + CONV_CH:]
    u = ua * _sigmoid(ug)
    return qs, ks, vs, u


def _p_inproj_kernel(x_ref, mod_ref, g_ref, w_ref, cos_ref, sin_ref,
                     q_ref, kb_ref, vt_ref, k_ref, v_ref, u_ref):
    d_model = x_ref.shape[-1]
    tm = x_ref.shape[1]
    shift = mod_ref[0, :, 0:d_model]
    scale = mod_ref[0, :, d_model:2 * d_model]
    qs, ks, vs, u = _inproj_math(x_ref[0], shift, scale, g_ref[...], w_ref, cos_ref[...], sin_ref[...],
                                 ATTN_SCALE * LOG2_E)
    for hh in range(N_HEADS):
        q_ref[0, hh] = qs[hh].astype(BF16)
        kb_ref[0, hh] = ks[hh].astype(BF16)
        k_ref[0, pl.ds(hh, tm, stride=N_HEADS), :] = ks[hh]
        v_ref[0, pl.ds(hh, tm, stride=N_HEADS), :] = vs[hh]
        vt_ref[0, hh, 0] = vs[hh].T.astype(BF16)
    u_ref[0] = u


def _p_inproj(x, mod3, g_pre, w_in_b, cos, sin_signed, tm):
    b, s, d = x.shape
    nt = s // tm
    d_in = w_in_b.shape[1]
    return pl.pallas_call(
        _p_inproj_kernel,
        grid=(b, nt),
        in_specs=[
            pl.BlockSpec((1, tm, d), lambda bi, i: (bi, i, 0)),
            pl.BlockSpec((1, 1, mod3.shape[-1]), lambda bi, i: (bi, 0, 0)),
            pl.BlockSpec((1, d), lambda bi, i: (0, 0)),
            pl.BlockSpec((d, d_in), lambda bi, i: (0, 0)),
            pl.BlockSpec((tm, LANES), lambda bi, i: (i, 0)),
            pl.BlockSpec((tm, LANES), lambda bi, i: (i, 0)),
        ],
        out_specs=[
            pl.BlockSpec((1, N_HEADS, tm, HEAD_W), lambda bi, i: (bi, 0, i, 0)),
            pl.BlockSpec((1, N_HEADS, tm, HEAD_W), lambda bi, i: (bi, 0, i, 0)),
            pl.BlockSpec((1, N_HEADS, 1, DV, tm), lambda bi, i: (bi, 0, i, 0, 0)),
            pl.BlockSpec((1, tm * N_HEADS, HEAD_W), lambda bi, i: (bi, i, 0)),
            pl.BlockSpec((1, tm * N_HEADS, DV), lambda bi, i: (bi, i, 0)),
            pl.BlockSpec((1, tm, CONV_CH), lambda bi, i: (bi, i, 0)),
        ],
        out_shape=[
            jax.ShapeDtypeStruct((b, N_HEADS, s, HEAD_W), BF16),
            jax.ShapeDtypeStruct((b, N_HEADS, s, HEAD_W), BF16),
            jax.ShapeDtypeStruct((b, N_HEADS, nt, DV, tm), BF16),
            jax.ShapeDtypeStruct((b, s * N_HEADS, HEAD_W), F32),
            jax.ShapeDtypeStruct((b, s * N_HEADS, DV), F32),
            jax.ShapeDtypeStruct((b, s, CONV_CH), F32),
        ],
        compiler_params=pltpu.CompilerParams(
            dimension_semantics=("parallel", "arbitrary"), vmem_limit_bytes=VMEM_LIMIT),
        name="p_inproj",
    )(x, mod3, g_pre, w_in_b, cos, sin_signed)


def _p_attn_kernel(lq1_ref, lk1_ref, lq2_ref, lk2_ref, g_ref, q_ref, k_ref, vt_ref, o_ref,
                   m_ref, l_ref, acc0_ref, acc1_ref, sa_ref, sb_ref, *, lam_init, tq):
    qi = pl.program_id(2)
    qt = q_ref[0, 0].astype(F32).T
    row = lax.broadcasted_iota(jnp.int32, qt.shape, 0)
    zero = jnp.zeros_like(qt)
    q2t = jnp.concatenate([jnp.where(row < DK, qt, zero), jnp.where(row >= DK, qt, zero)],
                          axis=1).astype(BF16)

    m_ref[...] = jnp.full(m_ref.shape, NEG_INF, F32)
    l_ref[...] = jnp.zeros(l_ref.shape, F32)
    acc0_ref[...] = jnp.zeros(acc0_ref.shape, F32)
    acc1_ref[...] = jnp.zeros(acc1_ref.shape, F32)

    def scores(c, dst_ref):
        start = pl.multiple_of(c * tq, tq)
        kc = k_ref[0, 0, pl.ds(start, tq), :]
        dst_ref[:, 0:2 * tq] = jnp.dot(kc, q2t, preferred_element_type=F32)

    def softmax_pv(src_ref, c, masked):
        st = src_ref[:, 0:2 * tq]
        vtc = vt_ref[0, 0, c]
        if masked:
            row = lax.broadcasted_iota(jnp.int32, st.shape, 0)
            col = lax.broadcasted_iota(jnp.int32, st.shape, 1)
            col = jnp.where(col >= tq, col - tq, col)
            st = jnp.where(row <= col, st, NEG_INF)
        m_old = m_ref[...]
        m_new = jnp.maximum(m_old, jnp.max(st, axis=0, keepdims=True))
        alpha = jnp.exp2(m_old - m_new)
        pb = jnp.exp2(st - m_new).astype(BF16)
        m_ref[...] = m_new
        vt1 = jnp.concatenate([vtc, jnp.ones((2 * SUBLANES, vtc.shape[1]), BF16)], axis=0)
        pv0 = jnp.dot(vt1, pb[:, :tq], preferred_element_type=F32)
        pv1 = jnp.dot(vt1, pb[:, tq:], preferred_element_type=F32)
        l_ref[...] = alpha * l_ref[...] + jnp.concatenate([pv0[DV:DV + 1, :], pv1[DV:DV + 1, :]], axis=1)
        acc0_ref[...] = acc0_ref[...] * alpha[:, :tq] + pv0[:DV, :]
        acc1_ref[...] = acc1_ref[...] * alpha[:, tq:] + pv1[:DV, :]

    bufs = (sa_ref, sb_ref)

    def run(c0, n, masked_last, prefetch_after):
        for t in range(n):
            if t + 1 < n or prefetch_after:
                scores(c0 + t + 1, bufs[(t + 1) % 2])
            softmax_pv(bufs[t % 2], c0 + t, masked=masked_last and t == n - 1)

    scores(0, sa_ref)

    def body(j, carry):
        run(ATTN_UNROLL * j, ATTN_UNROLL, masked_last=False, prefetch_after=True)
        return carry

    lax.fori_loop(0, qi // ATTN_UNROLL, body, 0)

    for r in range(ATTN_UNROLL):
        @pl.when(qi % ATTN_UNROLL == r)
        def _():
            run(qi - r, r + 1, masked_last=True, prefetch_after=False)

    lam = _lambda(lq1_ref[...], lk1_ref[...], lq2_ref[...], lk2_ref[...], lam_init)
    l = l_ref[...]
    ot = acc0_ref[...] * (1.0 / l[:, :tq]) - lam * (acc1_ref[...] * (1.0 / l[:, tq:]))
    ms = jnp.mean(ot * ot, axis=0, keepdims=True)
    ot = ot * lax.rsqrt(ms + NORM_EPS) * g_ref[...] * (1.0 - lam_init)
    o_ref[0] = ot.T.astype(BF16)


def _p_attn(lam_params, g_col, q, kb, vt, lam_init, tq):
    b, _, s, _ = q.shape
    nq = s // tq
    lam_spec = pl.BlockSpec((1, DK), lambda bi, hi, i: (0, 0))
    return pl.pallas_call(
        functools.partial(_p_attn_kernel, lam_init=lam_init, tq=tq),
        grid=(b, N_HEADS, nq),
        in_specs=[
            lam_spec, lam_spec, lam_spec, lam_spec,
            pl.BlockSpec((DV, 1), lambda bi, hi, i: (0, 0)),
            pl.BlockSpec((1, 1, tq, HEAD_W), lambda bi, hi, i: (bi, hi, i, 0)),
            pl.BlockSpec((1, 1, s, HEAD_W), lambda bi, hi, i: (bi, hi, 0, 0)),
            pl.BlockSpec((1, 1, nq, DV, tq), lambda bi, hi, i: (bi, hi, 0, 0, 0)),
        ],
        out_specs=pl.BlockSpec((1, tq, DV), lambda bi, hi, i: (bi, i, hi)),
        out_shape=jax.ShapeDtypeStruct((b, s, V_W), BF16),
        scratch_shapes=[
            pltpu.VMEM((1, 2 * tq), F32),
            pltpu.VMEM((1, 2 * tq), F32),
            pltpu.VMEM((DV, tq), F32),
            pltpu.VMEM((DV, tq), F32),
            pltpu.VMEM((tq, 2 * tq + SCORE_PAD), F32),
            pltpu.VMEM((tq, 2 * tq + SCORE_PAD), F32),
        ],
        compiler_params=pltpu.CompilerParams(
            dimension_semantics=("parallel", "parallel", "arbitrary"), vmem_limit_bytes=VMEM_LIMIT),
        name="p_attn",
    )(*lam_params, g_col, q, kb, vt)


def _post_math(x, mixb, mod, gains, w_out_ref, w_up_ref, w_down_ref):
    d = x.shape[-1]
    g_post_mix, g_pre_ffn, g_post_ffn = gains
    gt1 = mod[:, 2 * d:3 * d]
    sh2 = mod[:, 3 * d:4 * d]
    sc2 = mod[:, 4 * d:5 * d]
    gt2 = mod[:, 5 * d:6 * d]
    m = jnp.dot(mixb, w_out_ref[...], preferred_element_type=F32)
    x1 = x + gt1 * _rms(m, g_post_mix)
    h2 = _rms(x1, g_pre_ffn) * (1.0 + sc2) + sh2
    f = _mlp(h2.astype(BF16), w_up_ref, w_down_ref, w_up_ref.shape[1])
    return x1 + gt2 * _rms(f, g_post_ffn)


def _p_post_kernel(x_ref, o_ref, u_ref, uprev_ref, mod_ref, wdw_ref, bdw_ref, gcln_ref, bcln_ref,
                   gpm_ref, gpf_ref, gqf_ref, w_out_ref, w_up_ref, w_down_ref, y_ref, uext_ref, *, tm):
    i = pl.program_id(1)

    @pl.when(i == 0)
    def _():
        uext_ref[0:CONV_HALO, :] = jnp.zeros((CONV_HALO, CONV_CH), F32)

    @pl.when(i > 0)
    def _():
        uext_ref[0:CONV_HALO, :] = uprev_ref[0]

    uext_ref[CONV_HALO:, :] = u_ref[0]
    conv = _causal_conv(uext_ref[...], wdw_ref, bdw_ref[...], tm)
    cv = _conv_ln_swish(conv, gcln_ref[...], bcln_ref[...])
    mixb = jnp.concatenate([o_ref[0], cv.astype(BF16)], axis=-1)
    y_ref[0] = _post_math(x_ref[0], mixb, mod_ref[0], (gpm_ref[...], gpf_ref[...], gqf_ref[...]),
                          w_out_ref, w_up_ref, w_down_ref)


def _const_spec(shape, nargs):
    zeros = (0,) * len(shape)
    if nargs == 2:
        return pl.BlockSpec(shape, lambda a, b: zeros, pipeline_mode=pl.Buffered(1))
    return pl.BlockSpec(shape, lambda a: zeros, pipeline_mode=pl.Buffered(1))


def _p_post(x, o, u, mod3, w_dw, b_dw, g_cln, b_cln, g_post_mix, g_pre_ffn, g_post_ffn,
            w_out_b, w_up_b, w_down_b, tm):
    b, s, d = x.shape
    nt = s // tm
    d_ff = w_up_b.shape[1]
    halo_blocks = tm // CONV_HALO
    vec = lambda n: pl.BlockSpec((1, n), lambda bi, i: (0, 0))
    return pl.pallas_call(
        functools.partial(_p_post_kernel, tm=tm),
        grid=(b, nt),
        in_specs=[
            pl.BlockSpec((1, tm, d), lambda bi, i: (bi, i, 0)),
            pl.BlockSpec((1, tm, V_W), lambda bi, i: (bi, i, 0)),
            pl.BlockSpec((1, tm, CONV_CH), lambda bi, i: (bi, i, 0)),
            pl.BlockSpec((1, CONV_HALO, CONV_CH), lambda bi, i: (bi, jnp.maximum(i * halo_blocks - 1, 0), 0)),
            pl.BlockSpec((1, 1, mod3.shape[-1]), lambda bi, i: (bi, 0, 0)),
            pl.BlockSpec((CONV_K, CONV_CH), lambda bi, i: (0, 0)),
            vec(CONV_CH), vec(CONV_CH), vec(CONV_CH),
            vec(d), vec(d), vec(d),
            _const_spec((V_W + CONV_CH, d), 2),
            _const_spec((d, d_ff), 2),
            _const_spec((d_ff, d), 2),
        ],
        out_specs=pl.BlockSpec((1, tm, d), lambda bi, i: (bi, i, 0)),
        out_shape=jax.ShapeDtypeStruct((b, s, d), F32),
        scratch_shapes=[pltpu.VMEM((tm + CONV_HALO, CONV_CH), F32)],
        compiler_params=pltpu.CompilerParams(
            dimension_semantics=("parallel", "arbitrary"), vmem_limit_bytes=VMEM_LIMIT),
        name="p_post",
    )(x, o, u, u, mod3, w_dw, b_dw, g_cln, b_cln, g_post_mix, g_pre_ffn, g_post_ffn,
      w_out_b, w_up_b, w_down_b)


def _s_inproj_kernel(x_ref, mod_ref, g_ref, w_ref, cos_ref, sin_ref, q_ref, k_ref, v_ref, u_ref):
    d_model = x_ref.shape[-1]
    shift = mod_ref[:, 0:d_model]
    scale = mod_ref[:, d_model:2 * d_model]
    qs, ks, vs, u = _inproj_math(x_ref[...], shift, scale, g_ref[...], w_ref, cos_ref[...], sin_ref[...],
                                 ATTN_SCALE)
    for hh in range(N_HEADS):
        lo = hh * HEAD_W
        q_ref[:, lo:lo + HEAD_W] = qs[hh]
        k_ref[:, lo:lo + HEAD_W] = ks[hh]
        v_ref[:, lo:lo + DV] = vs[hh]
    u_ref[...] = u


def _s_inproj(x, mod, g_pre, w_in_b, cos_row, sin_row):
    rows, d = x.shape
    full = lambda a: pl.BlockSpec(a.shape, lambda i: (0,) * a.ndim)
    out = jax.ShapeDtypeStruct((rows, QK_W), F32)
    return pl.pallas_call(
        _s_inproj_kernel,
        grid=(1,),
        in_specs=[full(x), full(mod), full(g_pre), full(w_in_b), full(cos_row), full(sin_row)],
        out_specs=[pl.BlockSpec((rows, QK_W), lambda i: (0, 0))] * 4,
        out_shape=[out, out, out, out],
        compiler_params=pltpu.CompilerParams(vmem_limit_bytes=VMEM_LIMIT),
        name="s_inproj",
    )(x, mod, g_pre, w_in_b, cos_row, sin_row)


def _s_attn_kernel(pt_ref, lq1_ref, lk1_ref, lq2_ref, lk2_ref, q_ref, kn_ref, vn_ref, *rest,
                   lam_init, pps, spg):
    k_refs = rest[:spg * pps]
    v_refs = rest[spg * pps:2 * spg * pps]
    o_ref = rest[2 * spg * pps]
    m_ref, l_ref, acc_ref = rest[2 * spg * pps + 1:]
    step = pl.program_id(1)
    groups = 2 * N_HEADS
    n_rows = pps * k_refs[0].shape[1]
    grp = lax.broadcasted_iota(jnp.int32, (groups, HEAD_W), 0)
    lane = lax.broadcasted_iota(jnp.int32, (groups, HEAD_W), 1)
    own_component = (lane >= DK) == ((grp & 1) == 1)
    row_grp = lax.broadcasted_iota(jnp.int32, (groups, n_rows), 0)
    col = lax.broadcasted_iota(jnp.int32, (groups, n_rows), 1)
    own_head = (col & (N_HEADS - 1)) == (row_grp >> 1)

    def per_group(row):
        return jnp.concatenate([row[:, (g // 2) * HEAD_W:(g // 2 + 1) * HEAD_W] for g in range(groups)], axis=0)

    for r in range(spg):
        qg = jnp.where(own_component, per_group(q_ref[0, r:r + 1, :]), 0.0)

        @pl.when(step == 0)
        def _():
            m_ref[r] = jnp.sum(qg * per_group(kn_ref[0, r:r + 1, :]), axis=-1, keepdims=True)
            l_ref[r] = jnp.ones(l_ref.shape[1:], F32)
            acc_ref[r] = per_group(vn_ref[0, r:r + 1, :])

        kd = jnp.concatenate([kr[0] for kr in k_refs[r * pps:(r + 1) * pps]], axis=0).astype(BF16)
        vd = jnp.concatenate([vr[0] for vr in v_refs[r * pps:(r + 1) * pps]], axis=0).astype(BF16)
        s = lax.dot_general(qg.astype(BF16), kd, (((1,), (1,)), ((), ())), preferred_element_type=F32)
        s = jnp.where(own_head, s, NEG_INF)
        m_old = m_ref[r]
        m_new = jnp.maximum(m_old, jnp.max(s, axis=-1, keepdims=True))
        alpha = jnp.exp(m_old - m_new)
        p = jnp.exp(s - m_new)
        l_ref[r] = alpha * l_ref[r] + jnp.sum(p, axis=-1, keepdims=True)
        m_ref[r] = m_new
        acc_ref[r] = acc_ref[r] * alpha + jnp.dot(p.astype(BF16), vd, preferred_element_type=F32)

    @pl.when(step == pl.num_programs(1) - 1)
    def _():
        lam = _lambda(lq1_ref[...], lk1_ref[...], lq2_ref[...], lk2_ref[...], lam_init)
        for r in range(spg):
            on = acc_ref[r] * (1.0 / l_ref[r])
            for hh in range(N_HEADS):
                o_ref[0, r:r + 1, hh * DV:(hh + 1) * DV] = on[2 * hh:2 * hh + 1, :] - lam * on[2 * hh + 1:2 * hh + 2, :]


def _s_attn(page_table, lam_params, q, k_new, v_new, cache_k, cache_v, lam_init):
    db, n_pages = page_table.shape
    pps = math.gcd(PAGES_PER_STEP, n_pages)
    spg = math.gcd(SEQS_PER_STEP, db)
    page_rows = cache_k.shape[1]
    q3, kn3, vn3 = (a.reshape(db // spg, spg, a.shape[-1]) for a in (q, k_new, v_new))
    lam_spec = pl.BlockSpec((1, DK), lambda b, s, pt: (0, 0))
    row_spec = pl.BlockSpec((1, spg, QK_W), lambda b, s, pt: (b, 0, 0))

    def page_spec(r, i):
        return pl.BlockSpec((1, page_rows, HEAD_W), lambda b, s, pt: (pt[b * spg + r, s * pps + i], 0, 0))

    page_specs = [page_spec(r, i) for r in range(spg) for i in range(pps)]
    grid_spec = pltpu.PrefetchScalarGridSpec(
        num_scalar_prefetch=1,
        grid=(db // spg, n_pages // pps),
        in_specs=[lam_spec] * 4 + [row_spec] * 3 + page_specs * 2,
        out_specs=pl.BlockSpec((1, spg, V_W), lambda b, s, pt: (b, 0, 0)),
        scratch_shapes=[
            pltpu.VMEM((spg, 2 * N_HEADS, 1), F32),
            pltpu.VMEM((spg, 2 * N_HEADS, 1), F32),
            pltpu.VMEM((spg, 2 * N_HEADS, DV), F32),
        ],
    )
    n_page_args = spg * pps
    out = pl.pallas_call(
        functools.partial(_s_attn_kernel, lam_init=lam_init, pps=pps, spg=spg),
        grid_spec=grid_spec,
        out_shape=jax.ShapeDtypeStruct((db // spg, spg, V_W), F32),
        compiler_params=pltpu.CompilerParams(
            dimension_semantics=("parallel", "arbitrary"), vmem_limit_bytes=VMEM_LIMIT),
        name="s_attn",
    )(page_table, *lam_params, q3, kn3, vn3, *([cache_k] * n_page_args), *([cache_v] * n_page_args))
    return out.reshape(db, V_W)


def _s_post_kernel(x_ref, o_ref, u_ref, st_ref, mod_ref, gsub_ref, wdw_ref, bdw_ref, gcln_ref, bcln_ref,
                   gpm_ref, gpf_ref, gqf_ref, w_out_ref, w_up_ref, w_down_ref, y_ref, *, lam_init):
    o = o_ref[...]
    heads = []
    for hh in range(N_HEADS):
        oh = o[:, hh * DV:(hh + 1) * DV]
        heads.append(_rms(oh, gsub_ref[...]) * (1.0 - lam_init))
    conv = bdw_ref[...] + wdw_ref[CONV_K - 1:CONV_K, :] * u_ref[...]
    for j in range(CONV_K - 1):
        conv = conv + wdw_ref[j:j + 1, :] * st_ref[j]
    cv = _conv_ln_swish(conv, gcln_ref[...], bcln_ref[...])
    mixb = jnp.concatenate(heads + [cv], axis=-1).astype(BF16)
    y_ref[...] = _post_math(x_ref[...], mixb, mod_ref[...], (gpm_ref[...], gpf_ref[...], gqf_ref[...]),
                            w_out_ref, w_up_ref, w_down_ref)


def _s_post(x, o, u, state_t, mod, g_subln, w_dw, b_dw, g_cln, b_cln, g_post_mix, g_pre_ffn, g_post_ffn,
            w_out_b, w_up_b, w_down_b, lam_init):
    rows, d = x.shape
    args = (x, o, u, state_t, mod, g_subln, w_dw, b_dw, g_cln, b_cln, g_post_mix, g_pre_ffn, g_post_ffn,
            w_out_b, w_up_b, w_down_b)
    return pl.pallas_call(
        functools.partial(_s_post_kernel, lam_init=lam_init),
        grid=(1,),
        in_specs=[_const_spec(a.shape, 1) for a in args],
        out_specs=pl.BlockSpec((rows, d), lambda i: (0, 0)),
        out_shape=jax.ShapeDtypeStruct((rows, d), F32),
        compiler_params=pltpu.CompilerParams(vmem_limit_bytes=VMEM_LIMIT),
        name="s_post",
    )(*args)


def _rope_tables(pos):
    half = DK // 2
    inv_freq = ROPE_THETA ** (-jnp.arange(half, dtype=F32) / half)
    ang = pos.astype(F32)[:, None] * inv_freq[None, :]
    cos, sin = jnp.cos(ang), jnp.sin(ang)
    reps = LANES // DK
    cos128 = jnp.tile(jnp.concatenate([cos, cos], axis=-1), (1, reps))
    sin128 = jnp.tile(jnp.concatenate([-sin, sin], axis=-1), (1, reps))
    return cos128, sin128


def kernel(x_prompt, x_sample, c_prompt, c_sample, cache_k, cache_v, state_conv, page_table, w_in, w_out, lam_q1, lam_k1, lam_q2, lam_k2, g_subln, w_dw, b_dw, g_cln, b_cln, g_pre_mix, g_post_mix, g_pre_ffn, g_post_ffn, w_ada, b_ada, w_up, w_down):
    b, s, d = x_prompt.shape
    db, t, _ = x_sample.shape
    depth = w_in.shape[0]
    assert depth == 1 and t == 1, "kernel supports one layer and one new token per sample sequence"
    n_phys, page = cache_k.shape[1], cache_k.shape[2]
    past = page_table.shape[1] * page
    tm = min(ROW_TILE, s)
    assert s % tm == 0 and tm % CONV_HALO == 0
    lam_init = 0.8 - 0.6 * math.exp(-0.3 * 0)

    w_in_b = w_in[0].astype(BF16)
    w_out_b = w_out[0].astype(BF16)
    w_up_b = w_up[0].astype(BF16)
    w_down_b = w_down[0].astype(BF16)
    w_ada_b = w_ada[0].astype(BF16)
    row = lambda a: a.reshape(1, -1)
    lam_params = (row(lam_q1[0]), row(lam_k1[0]), row(lam_q2[0]), row(lam_k2[0]))
    g_pre, g_pm, g_pf, g_qf = row(g_pre_mix[0]), row(g_post_mix[0]), row(g_pre_ffn[0]), row(g_post_ffn[0])
    b_dw2, g_cln2, b_cln2 = row(b_dw[0]), row(g_cln[0]), row(b_cln[0])

    mod_p = _ada(c_prompt, w_ada_b, row(b_ada[0]))
    mod_s = _ada(c_sample, w_ada_b, row(b_ada[0]))
    mod_p3 = mod_p.reshape(b, 1, mod_p.shape[-1])

    cos_p, sin_p = _rope_tables(jnp.arange(s))
    q, kb, vt, k_p, v_p, u_p = _p_inproj(x_prompt, mod_p3, g_pre, w_in_b, cos_p, sin_p, tm)
    o_p = _p_attn(lam_params, g_subln[0].reshape(DV, 1), q, kb, vt, lam_init, tm)
    y_p = _p_post(x_prompt, o_p, u_p, mod_p3, w_dw[0], b_dw2, g_cln2, b_cln2, g_pm, g_pf, g_qf,
                  w_out_b, w_up_b, w_down_b, tm)

    cos_s, sin_s = _rope_tables(past + jnp.arange(t))
    xs = x_sample.reshape(db, d)
    q_s, k_s, v_s, u_s = _s_inproj(xs, mod_s, g_pre, w_in_b, cos_s, sin_s)
    ck = cache_k[0].reshape(n_phys, page * N_HEADS, HEAD_W)
    cv = cache_v[0].reshape(n_phys, page * N_HEADS, DV)
    o_s = _s_attn(page_table, lam_params, q_s, k_s, v_s, ck, cv, lam_init)
    state = state_conv[0]
    y_s = _s_post(xs, o_s, u_s, jnp.transpose(state, (1, 0, 2)), mod_s, row(g_subln[0]),
                  w_dw[0], b_dw2, g_cln2, b_cln2, g_pm, g_pf, g_qf, w_out_b, w_up_b, w_down_b, lam_init)

    k_prompt = k_p.reshape(1, b, s, N_HEADS, 2 * DK)
    v_prompt = v_p.reshape(1, b, s, N_HEADS, DV)
    conv_prompt = u_p[:, s - (CONV_K - 1):, :][None]
    k_sample = k_s.reshape(1, db, 1, N_HEADS, 2 * DK)
    v_sample = v_s.reshape(1, db, 1, N_HEADS, DV)
    conv_sample = jnp.concatenate([state[:, 1:, :], u_s[:, None, :]], axis=1)[None]
    return (y_p, y_s.reshape(db, 1, d), k_prompt, v_prompt, conv_prompt, k_sample, v_sample, conv_sample)
```
